```python
import jax, jax.numpy as jnp
from jax import lax
import numpy as np

D_MODEL = 2048
BATCH = 1
SEQ = 8192
DEPTH = 4

N_MIXERS = 2
POOL_WINDOWS = (2, 4, 8, 16)
N_POOL_GROUPS = 4
POOL_GROUP_DIM = D_MODEL // N_POOL_GROUPS
HEAD_DIM = 128
N_HEADS = D_MODEL // HEAD_DIM
Q_BLOCK = 128
D_FF = 5504
CONV_WIDTH = 3
LN_EPS = 1e-5
DEEPNORM_ALPHA = (2.0 * DEPTH) ** 0.25
DEEPNORM_BETA = (8.0 * DEPTH) ** -0.25
N_POOL_LAYERS = (DEPTH + 1) // 2
N_ATTN_LAYERS = DEPTH // 2

kernel_name = "hybrid_pool_stickbreak_convffn_deepnorm"


def layer_norm(x, g, b):
    xf = x.astype(jnp.float32)
    mu = jnp.mean(xf, axis=-1, keepdims=True)
    var = jnp.mean(jnp.square(xf - mu), axis=-1, keepdims=True)
    y = (xf - mu) * lax.rsqrt(var + LN_EPS)
    return (y * g.astype(jnp.float32) + b.astype(jnp.float32)).astype(x.dtype)


def pool_mixer(x, w_groups, scale):
    B, S, D = x.shape
    xf = x.astype(jnp.float32)
    cs = jnp.cumsum(xf, axis=1).reshape(B, S, N_POOL_GROUPS, POOL_GROUP_DIM)
    xg = xf.reshape(B, S, N_POOL_GROUPS, POOL_GROUP_DIM)
    t1 = jnp.arange(1, S + 1)
    outs = []
    for g, w in enumerate(POOL_WINDOWS):
        c = cs[:, :, g]
        shifted = jnp.pad(c, ((0, 0), (w, 0), (0, 0)))[:, :S]
        cnt = jnp.minimum(t1, w).astype(jnp.float32)[None, :, None]
        outs.append((c - shifted) / cnt - xg[:, :, g])
    pooled = jnp.stack(outs, axis=2).astype(x.dtype)
    y = jnp.einsum('bsgc,gcd->bsgd', pooled, w_groups).reshape(B, S, D)
    return y * scale


def stick_breaking_attention(x, w_qkv, w_o):
    B, S, D = x.shape
    qkv = x @ w_qkv
    q, k, v = jnp.split(qkv, 3, axis=-1)
    nb = S // Q_BLOCK
    q_blocks = q.reshape(B, nb, Q_BLOCK, N_HEADS, HEAD_DIM).transpose(1, 0, 3, 2, 4)
    k = k.reshape(B, S, N_HEADS, HEAD_DIM).transpose(0, 2, 1, 3)
    v = v.reshape(B, S, N_HEADS, HEAD_DIM).transpose(0, 2, 1, 3)
    k_pos = jnp.arange(S)
    scale = HEAD_DIM ** -0.5

    def block(args):
        qb, bi = args
        z = jnp.einsum('bhqd,bhkd->bhqk', qb, k).astype(jnp.float32) * scale
        q_pos = bi * Q_BLOCK + jnp.arange(Q_BLOCK)
        mask = k_pos[None, :] < q_pos[:, None]
        log_not = jnp.where(mask, jax.nn.log_sigmoid(-z), 0.0)
        rest = lax.cumsum(log_not, axis=3, reverse=True)
        log_a = jnp.where(mask, z + rest, -jnp.inf)
        a = jnp.exp(log_a)
        return jnp.einsum('bhqk,bhkd->bhqd', a.astype(v.dtype), v)

    o = lax.map(block, (q_blocks, jnp.arange(nb)))
    o = o.transpose(1, 0, 3, 2, 4).reshape(B, S, D)
    return o @ w_o


def conv_ffn(x, w_up, conv_w, conv_b, w_down):
    S = x.shape[1]
    h = x @ w_up
    hp = jnp.pad(h, ((0, 0), (CONV_WIDTH - 1, 0), (0, 0)))
    hc = conv_b + conv_w[0] * hp[:, 0:S]
    for kk in range(1, CONV_WIDTH):
        hc = hc + conv_w[kk] * hp[:, kk:kk + S]
    gate, val = jnp.split(hc, 2, axis=-1)
    return (jax.nn.silu(gate) * val) @ w_down


def setup_inputs(seed: int = 0) -> dict:
    key = jax.random.key(seed)
    ks = jax.random.split(key, 14)
    f32 = jnp.float32
    nrm = lambda k, shp: jax.random.normal(k, shp, dtype=f32)
    x = nrm(ks[0], (BATCH, SEQ, D_MODEL))
    pool_w = nrm(ks[1], (N_POOL_LAYERS, N_POOL_GROUPS, POOL_GROUP_DIM, POOL_GROUP_DIM)) * (POOL_GROUP_DIM ** -0.5) * DEEPNORM_BETA
    pool_scale = 1.0 + 0.02 * nrm(ks[2], (N_POOL_LAYERS, D_MODEL))
    attn_w_qkv = nrm(ks[3], (N_ATTN_LAYERS, D_MODEL, 3 * D_MODEL)) * (D_MODEL ** -0.5)
    attn_w_o = nrm(ks[4], (N_ATTN_LAYERS, D_MODEL, D_MODEL)) * (D_MODEL ** -0.5) * DEEPNORM_BETA
    ffn_w_up = nrm(ks[5], (DEPTH, D_MODEL, 2 * D_FF)) * (D_MODEL ** -0.5)
    ffn_conv_w = nrm(ks[6], (DEPTH, CONV_WIDTH, 2 * D_FF)) * (CONV_WIDTH ** -0.5)
    ffn_conv_b = 0.02 * nrm(ks[7], (DEPTH, 2 * D_FF))
    ffn_w_down = nrm(ks[8], (DEPTH, D_FF, D_MODEL)) * (D_FF ** -0.5) * DEEPNORM_BETA
    ln_mix_g = 1.0 + 0.02 * nrm(ks[9], (DEPTH, D_MODEL))
    ln_mix_b = 0.02 * nrm(ks[10], (DEPTH, D_MODEL))
    ln_ffn_g = 1.0 + 0.02 * nrm(ks[11], (DEPTH, D_MODEL))
    ln_ffn_b = 0.02 * nrm(ks[12], (DEPTH, D_MODEL))
    return {"x": x, "pool_w": pool_w, "pool_scale": pool_scale,
            "attn_w_qkv": attn_w_qkv, "attn_w_o": attn_w_o,
            "ffn_w_up": ffn_w_up, "ffn_conv_w": ffn_conv_w, "ffn_conv_b": ffn_conv_b,
            "ffn_w_down": ffn_w_down, "ln_mix_g": ln_mix_g, "ln_mix_b": ln_mix_b,
            "ln_ffn_g": ln_ffn_g, "ln_ffn_b": ln_ffn_b}


def reference(x, pool_w, pool_scale, attn_w_qkv, attn_w_o, ffn_w_up, ffn_conv_w,
              ffn_conv_b, ffn_w_down, ln_mix_g, ln_mix_b, ln_ffn_g, ln_ffn_b):
    for i in range(DEPTH):
        j = i // N_MIXERS
        if i % N_MIXERS == 0:
            y = pool_mixer(x, pool_w[j], pool_scale[j])
        else:
            y = stick_breaking_attention(x, attn_w_qkv[j], attn_w_o[j])
        x = layer_norm(DEEPNORM_ALPHA * x + y, ln_mix_g[i], ln_mix_b[i])
        f = conv_ffn(x, ffn_w_up[i], ffn_conv_w[i], ffn_conv_b[i], ffn_w_down[i])
        x = layer_norm(DEEPNORM_ALPHA * x + f, ln_ffn_g[i], ln_ffn_b[i])
    return x
```

```python
import functools
import math

import jax
import jax.numpy as jnp
from jax import lax
from jax.experimental import pallas as pl
from jax.experimental.pallas import tpu as pltpu

D_MODEL = 2048
DEPTH = 4
POOL_WINDOWS = (2, 4, 8, 16)
N_POOL_GROUPS = len(POOL_WINDOWS)
POOL_GROUP_DIM = D_MODEL // N_POOL_GROUPS
HEAD_DIM = 128
N_HEADS = D_MODEL // HEAD_DIM
D_FF = 5504
CONV_WIDTH = 3
LN_EPS = 1e-5
DEEPNORM_ALPHA = (2.0 * DEPTH) ** 0.25

V7X_MXU_DIM = 256
V7X_BF16_SUBLANES = 16
V7X_VMEM_LIMIT_BYTES = 56 * 1024 * 1024

HALO_ROWS = V7X_BF16_SUBLANES
POOL_ROWS = 512
QKV_ROWS, QKV_COLS = 1024, 512
ATTN_BLOCK = V7X_MXU_DIM
PROJ_ROWS = 256
FF_PAD = 22 * V7X_MXU_DIM
UP_ROWS, UP_COLS = 1024, 512
DOWN_ROWS, DOWN_K = 512, 512

LOG2_E = math.log2(math.e)


def _params(*semantics):
    return pltpu.CompilerParams(dimension_semantics=semantics,
                                vmem_limit_bytes=V7X_VMEM_LIMIT_BYTES)


def _residual_layer_norm(x, y, g, b):
    v = DEEPNORM_ALPHA * x + y
    mu = jnp.mean(v, axis=-1, keepdims=True)
    c = v - mu
    var = jnp.mean(c * c, axis=-1, keepdims=True)
    return c * lax.rsqrt(var + LN_EPS) * g + b


def _prev_halo_index(rows):
    per_tile = rows // HALO_ROWS
    return lambda i, *_: (jnp.maximum(i * per_tile - 1, 0), 0)


def _pool_ln_kernel(x_ref, halo_ref, w_ref, scale_ref, g_ref, b_ref, of_ref, ob_ref):
    i = pl.program_id(0)
    rows = x_ref.shape[0]
    x = x_ref[...]
    halo = jnp.where(i == 0, 0.0, halo_ref[...])
    t1 = i * rows + lax.broadcasted_iota(jnp.int32, (rows, 1), 0) + 1
    ys = []
    for grp, win in enumerate(POOL_WINDOWS):
        cols = slice(grp * POOL_GROUP_DIM, (grp + 1) * POOL_GROUP_DIM)
        xg = x[:, cols]
        s = jnp.concatenate([halo[:, cols], xg], axis=0)
        span = 1
        while span < win:
            s = s + pltpu.roll(s, span, axis=0)
            span *= 2
        cnt = jnp.minimum(t1, win).astype(jnp.float32)
        pooled = s[HALO_ROWS:, :] / cnt - xg
        ys.append(jnp.dot(pooled.astype(jnp.bfloat16), w_ref[grp],
                          preferred_element_type=jnp.float32))
    y = jnp.concatenate(ys, axis=1) * scale_ref[...]
    out = _residual_layer_norm(x, y, g_ref[...], b_ref[...])
    of_ref[...] = out
    ob_ref[...] = out.astype(jnp.bfloat16)


def _pool_ln(x, w, scale, g, b):
    seq = x.shape[0]
    row = lambda i: (i, 0)
    fixed = lambda i: (0, 0)
    return pl.pallas_call(
        _pool_ln_kernel,
        grid=(seq // POOL_ROWS,),
        in_specs=[pl.BlockSpec((POOL_ROWS, D_MODEL), row),
                  pl.BlockSpec((HALO_ROWS, D_MODEL), _prev_halo_index(POOL_ROWS)),
                  pl.BlockSpec(w.shape, lambda i: (0, 0, 0)),
                  pl.BlockSpec((1, D_MODEL), fixed),
                  pl.BlockSpec((1, D_MODEL), fixed),
                  pl.BlockSpec((1, D_MODEL), fixed)],
        out_specs=[pl.BlockSpec((POOL_ROWS, D_MODEL), row),
                   pl.BlockSpec((POOL_ROWS, D_MODEL), row)],
        out_shape=[jax.ShapeDtypeStruct((seq, D_MODEL), jnp.float32),
                   jax.ShapeDtypeStruct((seq, D_MODEL), jnp.bfloat16)],
        compiler_params=_params("arbitrary"),
        name="pool_ln",
    )(x, x, w, scale, g, b)


def _qkv_kernel(x_ref, w_ref, o_ref, *, q_col_tiles, q_scale):
    j = pl.program_id(1)
    h = jnp.dot(x_ref[...], w_ref[...], preferred_element_type=jnp.float32)
    h = h * jnp.where(j < q_col_tiles, q_scale, 1.0)
    o_ref[...] = h.astype(jnp.bfloat16)


def _qkv(xb, w):
    seq = xb.shape[0]
    n_out = w.shape[1]
    kern = functools.partial(_qkv_kernel, q_col_tiles=D_MODEL // QKV_COLS,
                             q_scale=HEAD_DIM ** -0.5 * LOG2_E)
    return pl.pallas_call(
        kern,
        grid=(seq // QKV_ROWS, n_out // QKV_COLS),
        in_specs=[pl.BlockSpec((QKV_ROWS, D_MODEL), lambda i, j: (i, 0)),
                  pl.BlockSpec((D_MODEL, QKV_COLS), lambda i, j: (0, j))],
        out_specs=pl.BlockSpec((QKV_ROWS, QKV_COLS), lambda i, j: (i, j)),
        out_shape=jax.ShapeDtypeStruct((seq, n_out), jnp.bfloat16),
        compiler_params=_params("arbitrary", "arbitrary"),
        name="qkv",
    )(xb, w)


def _attn_kernel(q_ref, k_ref, v_ref, tri_ref, o_ref):
    i = pl.program_id(1)
    blk = q_ref.shape[0]
    q = q_ref[...]
    tri = tri_ref[...]
    r = lax.broadcasted_iota(jnp.int32, (blk, blk), 0)
    c = lax.broadcasted_iota(jnp.int32, (blk, blk), 1)
    causal = c < r

    def visit(kb, carry, acc, masked):
        start = pl.multiple_of(kb * blk, blk)
        kblk = k_ref[pl.ds(start, blk), :]
        vblk = v_ref[pl.ds(start, blk), :]
        z = lax.dot_general(q, kblk, (((1,), (1,)), ((), ())),
                            preferred_element_type=jnp.float32)
        l = jnp.minimum(-z, 0.0) - jnp.log2(1.0 + jnp.exp2(-jnp.abs(z)))
        if masked:
            l = jnp.where(causal, l, 0.0)
        lb = l.astype(jnp.bfloat16)
        cs = jnp.dot(lb, tri, preferred_element_type=jnp.float32)
        a = jnp.exp2(z + l + cs + carry)
        if masked:
            a = jnp.where(causal, a, 0.0)
        acc = acc + jnp.dot(a.astype(jnp.bfloat16), vblk, preferred_element_type=jnp.float32)
        carry = carry + cs[:, 0:1] + lb[:, 0:1].astype(jnp.float32)
        return carry, acc

    carry0 = jnp.zeros((blk, 1), jnp.float32)
    acc0 = jnp.zeros((blk, q_ref.shape[1]), jnp.float32)
    carry, acc = visit(i, carry0, acc0, True)

    def body(n, state):
        return visit(i - 1 - n, state[0], state[1], False)

    _, acc = lax.fori_loop(0, i, body, (carry, acc))
    o_ref[...] = acc.astype(jnp.bfloat16)


def _attention(qkv, tri):
    seq = qkv.shape[0]
    return pl.pallas_call(
        _attn_kernel,
        grid=(N_HEADS, seq // ATTN_BLOCK),
        in_specs=[pl.BlockSpec((ATTN_BLOCK, HEAD_DIM), lambda h, i: (i, h)),
                  pl.BlockSpec((seq, HEAD_DIM), lambda h, i: (0, N_HEADS + h)),
                  pl.BlockSpec((seq, HEAD_DIM), lambda h, i: (0, 2 * N_HEADS + h)),
                  pl.BlockSpec((ATTN_BLOCK, ATTN_BLOCK), lambda h, i: (0, 0))],
        out_specs=pl.BlockSpec((ATTN_BLOCK, HEAD_DIM), lambda h, i: (i, h)),
        out_shape=jax.ShapeDtypeStruct((seq, D_MODEL), jnp.bfloat16),
        compiler_params=_params("arbitrary", "arbitrary"),
        name="stickbreak_attn",
    )(qkv, qkv, qkv, tri)


def _proj_ln_kernel(o_ref, x_ref, w_ref, g_ref, b_ref, of_ref, ob_ref):
    y = jnp.dot(o_ref[...], w_ref[...], preferred_element_type=jnp.float32)
    out = _residual_layer_norm(x_ref[...], y, g_ref[...], b_ref[...])
    of_ref[...] = out
    ob_ref[...] = out.astype(jnp.bfloat16)


def _proj_ln(o, x, w, g, b):
    seq = x.shape[0]
    row = lambda i: (i, 0)
    fixed = lambda i: (0, 0)
    return pl.pallas_call(
        _proj_ln_kernel,
        grid=(seq // PROJ_ROWS,),
        in_specs=[pl.BlockSpec((PROJ_ROWS, D_MODEL), row),
                  pl.BlockSpec((PROJ_ROWS, D_MODEL), row),
                  pl.BlockSpec((D_MODEL, D_MODEL), fixed),
                  pl.BlockSpec((1, D_MODEL), fixed),
                  pl.BlockSpec((1, D_MODEL), fixed)],
        out_specs=[pl.BlockSpec((PROJ_ROWS, D_MODEL), row),
                   pl.BlockSpec((PROJ_ROWS, D_MODEL), row)],
        out_shape=[jax.ShapeDtypeStruct((seq, D_MODEL), jnp.float32),
                   jax.ShapeDtypeStruct((seq, D_MODEL), jnp.bfloat16)],
        compiler_params=_params("arbitrary"),
        name="proj_ln",
    )(o, x, w, g, b)


def _causal_taps(h, halo):
    rows8 = lax.broadcasted_iota(jnp.int32, (8, 1), 0)
    top1 = pltpu.roll(halo, 1, axis=0)[0:8]
    top2 = pltpu.roll(halo, 2, axis=0)[0:8]
    r1 = pltpu.roll(h, 1, axis=0)
    r2 = pltpu.roll(h, 2, axis=0)
    h1 = jnp.concatenate([jnp.where(rows8 < 1, top1, r1[0:8]), r1[8:]], axis=0)
    h2 = jnp.concatenate([jnp.where(rows8 < 2, top2, r2[0:8]), r2[8:]], axis=0)
    return h1, h2


def _ffn_up_kernel(x_ref, halo_ref, wg_ref, wv_ref, cwg_ref, cwv_ref, cbg_ref, cbv_ref, o_ref):
    i = pl.program_id(0)
    x = x_ref[...]
    xh = halo_ref[...]

    def conv_branch(w_ref, cw_ref, cb_ref):
        w = w_ref[...]
        h = jnp.dot(x, w, preferred_element_type=jnp.float32)
        hh = jnp.dot(xh, w, preferred_element_type=jnp.float32)
        hh = jnp.where(i == 0, 0.0, hh)
        h1, h2 = _causal_taps(h, hh)
        cw = cw_ref[...]
        return cb_ref[...] + cw[0:1] * h2 + cw[1:2] * h1 + cw[2:3] * h

    gate = conv_branch(wg_ref, cwg_ref, cbg_ref)
    val = conv_branch(wv_ref, cwv_ref, cbv_ref)
    o_ref[...] = (gate * jax.nn.sigmoid(gate) * val).astype(jnp.bfloat16)


def _ffn_up(xb, wg, wv, cwg, cwv, cbg, cbv):
    seq = xb.shape[0]
    col = lambda i, j: (0, j)
    return pl.pallas_call(
        _ffn_up_kernel,
        grid=(seq // UP_ROWS, FF_PAD // UP_COLS),
        in_specs=[pl.BlockSpec((UP_ROWS, D_MODEL), lambda i, j: (i, 0)),
                  pl.BlockSpec((HALO_ROWS, D_MODEL), _prev_halo_index(UP_ROWS)),
                  pl.BlockSpec((D_MODEL, UP_COLS), col),
                  pl.BlockSpec((D_MODEL, UP_COLS), col),
                  pl.BlockSpec((CONV_WIDTH, UP_COLS), col),
                  pl.BlockSpec((CONV_WIDTH, UP_COLS), col),
                  pl.BlockSpec((1, UP_COLS), col),
                  pl.BlockSpec((1, UP_COLS), col)],
        out_specs=pl.BlockSpec((UP_ROWS, UP_COLS), lambda i, j: (i, j)),
        out_shape=jax.ShapeDtypeStruct((seq, FF_PAD), jnp.bfloat16),
        compiler_params=_params("arbitrary", "arbitrary"),
        name="ffn_up",
    )(xb, xb, wg, wv, cwg, cwv, cbg, cbv)


def _ffn_down_ln_kernel(a_ref, x_ref, w_ref, g_ref, b_ref, of_ref, ob_ref, acc_ref):
    k = pl.program_id(1)
    part = jnp.dot(a_ref[...], w_ref[...], preferred_element_type=jnp.float32)

    @pl.when(k == 0)
    def _():
        acc_ref[...] = part

    @pl.when(k > 0)
    def _():
        acc_ref[...] += part

    @pl.when(k == pl.num_programs(1) - 1)
    def _():
        out = _residual_layer_norm(x_ref[...], acc_ref[...], g_ref[...], b_ref[...])
        of_ref[...] = out
        ob_ref[...] = out.astype(jnp.bfloat16)


def _ffn_down_ln(act, x, w, g, b):
    seq = x.shape[0]
    row = lambda i, k: (i, 0)
    fixed = lambda i, k: (0, 0)
    return pl.pallas_call(
        _ffn_down_ln_kernel,
        grid=(seq // DOWN_ROWS, FF_PAD // DOWN_K),
        in_specs=[pl.BlockSpec((DOWN_ROWS, DOWN_K), lambda i, k: (i, k)),
                  pl.BlockSpec((DOWN_ROWS, D_MODEL), row),
                  pl.BlockSpec((DOWN_K, D_MODEL), lambda i, k: (k, 0)),
                  pl.BlockSpec((1, D_MODEL), fixed),
                  pl.BlockSpec((1, D_MODEL), fixed)],
        out_specs=[pl.BlockSpec((DOWN_ROWS, D_MODEL), row),
                   pl.BlockSpec((DOWN_ROWS, D_MODEL), row)],
        out_shape=[jax.ShapeDtypeStruct((seq, D_MODEL), jnp.float32),
                   jax.ShapeDtypeStruct((seq, D_MODEL), jnp.bfloat16)],
        scratch_shapes=[pltpu.VMEM((DOWN_ROWS, D_MODEL), jnp.float32)],
        compiler_params=_params("arbitrary", "arbitrary"),
        name="ffn_down_ln",
    )(act, x, w, g, b)


def _pad_ff(a, axis):
    pad = [(0, 0)] * a.ndim
    pad[axis] = (0, FF_PAD - D_FF)
    return jnp.pad(a, pad)


def kernel(x, pool_w, pool_scale, attn_w_qkv, attn_w_o, ffn_w_up, ffn_conv_w, ffn_conv_b,
           ffn_w_down, ln_mix_g, ln_mix_b, ln_ffn_g, ln_ffn_b):
    batch, seq, _ = x.shape
    bf16 = jnp.bfloat16
    vec = lambda a: a.reshape(1, -1)
    tri = jnp.tril(jnp.ones((ATTN_BLOCK, ATTN_BLOCK), bf16), k=-1)

    outs = []
    for bi in range(batch):
        xf = x[bi]
        xb = None
        for i in range(DEPTH):
            j = i // 2
            g, b = vec(ln_mix_g[i]), vec(ln_mix_b[i])
            if i % 2 == 0:
                xf, xb = _pool_ln(xf, pool_w[j].astype(bf16), vec(pool_scale[j]), g, b)
            else:
                qkv = _qkv(xb, attn_w_qkv[j].astype(bf16))
                o = _attention(qkv, tri)
                xf, xb = _proj_ln(o, xf, attn_w_o[j].astype(bf16), g, b)
            w_up = ffn_w_up[i]
            cw, cb = ffn_conv_w[i], ffn_conv_b[i]
            act = _ffn_up(xb,
                          _pad_ff(w_up[:, :D_FF].astype(bf16), 1),
                          _pad_ff(w_up[:, D_FF:].astype(bf16), 1),
                          _pad_ff(cw[:, :D_FF], 1), _pad_ff(cw[:, D_FF:], 1),
                          _pad_ff(vec(cb[:D_FF]), 1), _pad_ff(vec(cb[D_FF:]), 1))
            xf, xb = _ffn_down_ln(act, xf, _pad_ff(ffn_w_down[i].astype(bf16), 0),
                                  vec(ln_ffn_g[i]), vec(ln_ffn_b[i]))
        outs.append(xf)
    return jnp.stack(outs, axis=0)
```

```python
import functools
import math

import jax
import jax.numpy as jnp
from jax import lax
from jax.experimental import pallas as pl
from jax.experimental.pallas import tpu as pltpu

D_MODEL = 2048
DEPTH = 4
POOL_WINDOWS = (2, 4, 8, 16)
N_POOL_GROUPS = len(POOL_WINDOWS)
POOL_GROUP_DIM = D_MODEL // N_POOL_GROUPS
HEAD_DIM = 128
N_HEADS = D_MODEL // HEAD_DIM
D_FF = 5504
CONV_WIDTH = 3
LN_EPS = 1e-5
DEEPNORM_ALPHA = (2.0 * DEPTH) ** 0.25

V7X_MXU_DIM = 256
V7X_BF16_SUBLANES = 16
V7X_VMEM_LIMIT_BYTES = 56 * 1024 * 1024

HALO_ROWS = V7X_BF16_SUBLANES
POOL_ROWS = 512
QKV_ROWS, QKV_COLS = 1024, 512
ATTN_BLOCK = V7X_MXU_DIM
ATTN_HEADS = 4
PROJ_ROWS = 256
FF_PAD = 22 * V7X_MXU_DIM
UP_ROWS, UP_COLS = 1024, 512
DOWN_ROWS, DOWN_K = 512, 512

LOG2_E = math.log2(math.e)


def _params(*semantics):
    return pltpu.CompilerParams(dimension_semantics=semantics,
                                vmem_limit_bytes=V7X_VMEM_LIMIT_BYTES)


def _residual_layer_norm(x, y, g, b):
    v = DEEPNORM_ALPHA * x + y
    mu = jnp.mean(v, axis=-1, keepdims=True)
    c = v - mu
    var = jnp.mean(c * c, axis=-1, keepdims=True)
    return c * lax.rsqrt(var + LN_EPS) * g + b


def _prev_halo_index(rows):
    per_tile = rows // HALO_ROWS
    return lambda i, *_: (jnp.maximum(i * per_tile - 1, 0), 0)


def _pool_ln_kernel(x_ref, halo_ref, w_ref, scale_ref, g_ref, b_ref, of_ref, ob_ref):
    i = pl.program_id(0)
    rows = x_ref.shape[0]
    x = x_ref[...]
    halo = jnp.where(i == 0, 0.0, halo_ref[...])
    t1 = i * rows + lax.broadcasted_iota(jnp.int32, (rows, 1), 0) + 1
    ys = []
    for grp, win in enumerate(POOL_WINDOWS):
        cols = slice(grp * POOL_GROUP_DIM, (grp + 1) * POOL_GROUP_DIM)
        xg = x[:, cols]
        s = jnp.concatenate([halo[:, cols], xg], axis=0)
        span = 1
        while span < win:
            s = s + pltpu.roll(s, span, axis=0)
            span *= 2
        cnt = jnp.minimum(t1, win).astype(jnp.float32)
        pooled = s[HALO_ROWS:, :] / cnt - xg
        ys.append(jnp.dot(pooled.astype(jnp.bfloat16), w_ref[grp],
                          preferred_element_type=jnp.float32))
    y = jnp.concatenate(ys, axis=1) * scale_ref[...]
    out = _residual_layer_norm(x, y, g_ref[...], b_ref[...])
    of_ref[...] = out
    ob_ref[...] = out.astype(jnp.bfloat16)


def _pool_ln(x, w, scale, g, b):
    seq = x.shape[0]
    row = lambda i: (i, 0)
    fixed = lambda i: (0, 0)
    return pl.pallas_call(
        _pool_ln_kernel,
        grid=(seq // POOL_ROWS,),
        in_specs=[pl.BlockSpec((POOL_ROWS, D_MODEL), row),
                  pl.BlockSpec((HALO_ROWS, D_MODEL), _prev_halo_index(POOL_ROWS)),
                  pl.BlockSpec(w.shape, lambda i: (0, 0, 0)),
                  pl.BlockSpec((1, D_MODEL), fixed),
                  pl.BlockSpec((1, D_MODEL), fixed),
                  pl.BlockSpec((1, D_MODEL), fixed)],
        out_specs=[pl.BlockSpec((POOL_ROWS, D_MODEL), row),
                   pl.BlockSpec((POOL_ROWS, D_MODEL), row)],
        out_shape=[jax.ShapeDtypeStruct((seq, D_MODEL), jnp.float32),
                   jax.ShapeDtypeStruct((seq, D_MODEL), jnp.bfloat16)],
        compiler_params=_params("arbitrary"),
        name="pool_ln",
    )(x, x, w, scale, g, b)


def _qkv_kernel(x_ref, w_ref, o_ref, *, q_col_tiles, q_scale):
    j = pl.program_id(1)
    h = jnp.dot(x_ref[...], w_ref[...], preferred_element_type=jnp.float32)
    h = h * jnp.where(j < q_col_tiles, q_scale, 1.0)
    o_ref[...] = h.astype(jnp.bfloat16)


def _qkv(xb, w):
    seq = xb.shape[0]
    n_out = w.shape[1]
    kern = functools.partial(_qkv_kernel, q_col_tiles=D_MODEL // QKV_COLS,
                             q_scale=HEAD_DIM ** -0.5 * LOG2_E)
    return pl.pallas_call(
        kern,
        grid=(seq // QKV_ROWS, n_out // QKV_COLS),
        in_specs=[pl.BlockSpec((QKV_ROWS, D_MODEL), lambda i, j: (i, 0)),
                  pl.BlockSpec((D_MODEL, QKV_COLS), lambda i, j: (0, j))],
        out_specs=pl.BlockSpec((QKV_ROWS, QKV_COLS), lambda i, j: (i, j)),
        out_shape=jax.ShapeDtypeStruct((seq, n_out), jnp.bfloat16),
        compiler_params=_params("arbitrary", "arbitrary"),
        name="qkv",
    )(xb, w)


def _neg_abs(z):
    bits = lax.bitcast_convert_type(z, jnp.uint32) | jnp.uint32(0x80000000)
    return lax.bitcast_convert_type(bits, jnp.float32)


def _attn_kernel(q_ref, k_ref, v_ref, ntri_ref, o_ref, z_ref, a_ref, acc_ref, carry_ref):
    i = pl.program_id(1)
    blk = q_ref.shape[0]
    n_heads = q_ref.shape[1] // HEAD_DIM
    ntri = ntri_ref[...]
    head_lanes = [slice(hd * HEAD_DIM, (hd + 1) * HEAD_DIM) for hd in range(n_heads)]

    def logits(hd, kb):
        start = pl.multiple_of(kb * blk, blk)
        return lax.dot_general(q_ref[:, head_lanes[hd]], k_ref[pl.ds(start, blk), head_lanes[hd]],
                               (((1,), (1,)), ((), ())),
                               preferred_element_type=jnp.float32)

    def weigh(hd, kb):
        start = pl.multiple_of(kb * blk, blk)
        acc_ref[hd] += jnp.dot(a_ref[hd], v_ref[pl.ds(start, blk), head_lanes[hd]],
                               preferred_element_type=jnp.float32)

    def shape(hd):
        z = z_ref[hd]
        p = jnp.maximum(z, 0.0) + jnp.log2(1.0 + jnp.exp2(_neg_abs(z)))
        pb = p.astype(jnp.bfloat16)
        cs = jnp.dot(pb, ntri, preferred_element_type=jnp.float32)
        carry = carry_ref[hd]
        a_ref[hd] = jnp.exp2(z - p + cs + carry).astype(jnp.bfloat16)
        carry_ref[hd] = carry + cs[:, 0:1] - pb[:, 0:1].astype(jnp.float32)

    r = lax.broadcasted_iota(jnp.int32, (blk, blk), 0)
    c = lax.broadcasted_iota(jnp.int32, (blk, blk), 1)
    for hd in range(n_heads):
        z_ref[hd] = jnp.where(c < r, logits(hd, i), -jnp.inf)
        a_ref[hd] = jnp.zeros((blk, blk), jnp.bfloat16)
        acc_ref[hd] = jnp.zeros((blk, HEAD_DIM), jnp.float32)
        carry_ref[hd] = jnp.zeros((blk, 1), jnp.float32)

    def trip(t, _):
        for hd in range(n_heads):
            weigh(hd, jnp.minimum(i - t + 1, i))
        for hd in range(n_heads):
            shape(hd)
        for hd in range(n_heads):
            z_ref[hd] = logits(hd, jnp.maximum(i - t - 1, 0))
        return 0

    lax.fori_loop(0, i + 1, trip, 0)
    for hd in range(n_heads):
        weigh(hd, 0)
        o_ref[:, head_lanes[hd]] = acc_ref[hd].astype(jnp.bfloat16)


def _attention(qkv, ntri):
    seq = qkv.shape[0]
    width = ATTN_HEADS * HEAD_DIM
    groups = N_HEADS // ATTN_HEADS
    return pl.pallas_call(
        _attn_kernel,
        grid=(groups, seq // ATTN_BLOCK),
        in_specs=[pl.BlockSpec((ATTN_BLOCK, width), lambda h, i: (i, h)),
                  pl.BlockSpec((seq, width), lambda h, i: (0, groups + h)),
                  pl.BlockSpec((seq, width), lambda h, i: (0, 2 * groups + h)),
                  pl.BlockSpec((ATTN_BLOCK, ATTN_BLOCK), lambda h, i: (0, 0))],
        out_specs=pl.BlockSpec((ATTN_BLOCK, width), lambda h, i: (i, h)),
        out_shape=jax.ShapeDtypeStruct((seq, D_MODEL), jnp.bfloat16),
        scratch_shapes=[pltpu.VMEM((ATTN_HEADS, ATTN_BLOCK, ATTN_BLOCK), jnp.float32),
                        pltpu.VMEM((ATTN_HEADS, ATTN_BLOCK, ATTN_BLOCK), jnp.bfloat16),
                        pltpu.VMEM((ATTN_HEADS, ATTN_BLOCK, HEAD_DIM), jnp.float32),
                        pltpu.VMEM((ATTN_HEADS, ATTN_BLOCK, 1), jnp.float32)],
        compiler_params=_params("arbitrary", "arbitrary"),
        name="stickbreak_attn",
    )(qkv, qkv, qkv, ntri)


def _proj_ln_kernel(o_ref, x_ref, w_ref, g_ref, b_ref, of_ref, ob_ref):
    y = jnp.dot(o_ref[...], w_ref[...], preferred_element_type=jnp.float32)
    out = _residual_layer_norm(x_ref[...], y, g_ref[...], b_ref[...])
    of_ref[...] = out
    ob_ref[...] = out.astype(jnp.bfloat16)


def _proj_ln(o, x, w, g, b):
    seq = x.shape[0]
    row = lambda i: (i, 0)
    fixed = lambda i: (0, 0)
    return pl.pallas_call(
        _proj_ln_kernel,
        grid=(seq // PROJ_ROWS,),
        in_specs=[pl.BlockSpec((PROJ_ROWS, D_MODEL), row),
                  pl.BlockSpec((PROJ_ROWS, D_MODEL), row),
                  pl.BlockSpec((D_MODEL, D_MODEL), fixed),
                  pl.BlockSpec((1, D_MODEL), fixed),
                  pl.BlockSpec((1, D_MODEL), fixed)],
        out_specs=[pl.BlockSpec((PROJ_ROWS, D_MODEL), row),
                   pl.BlockSpec((PROJ_ROWS, D_MODEL), row)],
        out_shape=[jax.ShapeDtypeStruct((seq, D_MODEL), jnp.float32),
                   jax.ShapeDtypeStruct((seq, D_MODEL), jnp.bfloat16)],
        compiler_params=_params("arbitrary"),
        name="proj_ln",
    )(o, x, w, g, b)


def _causal_taps(h, halo):
    rows8 = lax.broadcasted_iota(jnp.int32, (8, 1), 0)
    top1 = pltpu.roll(halo, 1, axis=0)[0:8]
    top2 = pltpu.roll(halo, 2, axis=0)[0:8]
    r1 = pltpu.roll(h, 1, axis=0)
    r2 = pltpu.roll(h, 2, axis=0)
    h1 = jnp.concatenate([jnp.where(rows8 < 1, top1, r1[0:8]), r1[8:]], axis=0)
    h2 = jnp.concatenate([jnp.where(rows8 < 2, top2, r2[0:8]), r2[8:]], axis=0)
    return h1, h2


def _ffn_up_kernel(x_ref, halo_ref, wg_ref, wv_ref, cwg_ref, cwv_ref, cbg_ref, cbv_ref, o_ref):
    i = pl.program_id(0)
    x = x_ref[...]
    xh = halo_ref[...]

    def conv_branch(w_ref, cw_ref, cb_ref):
        w = w_ref[...]
        h = jnp.dot(x, w, preferred_element_type=jnp.float32)
        hh = jnp.dot(xh, w, preferred_element_type=jnp.float32)
        hh = jnp.where(i == 0, 0.0, hh)
        h1, h2 = _causal_taps(h, hh)
        cw = cw_ref[...]
        return cb_ref[...] + cw[0:1] * h2 + cw[1:2] * h1 + cw[2:3] * h

    gate = conv_branch(wg_ref, cwg_ref, cbg_ref)
    val = conv_branch(wv_ref, cwv_ref, cbv_ref)
    o_ref[...] = (gate * jax.nn.sigmoid(gate) * val).astype(jnp.bfloat16)


def _ffn_up(xb, wg, wv, cwg, cwv, cbg, cbv):
    seq = xb.shape[0]
    col = lambda i, j: (0, j)
    return pl.pallas_call(
        _ffn_up_kernel,
        grid=(seq // UP_ROWS, FF_PAD // UP_COLS),
        in_specs=[pl.BlockSpec((UP_ROWS, D_MODEL), lambda i, j: (i, 0)),
                  pl.BlockSpec((HALO_ROWS, D_MODEL), _prev_halo_index(UP_ROWS)),
                  pl.BlockSpec((D_MODEL, UP_COLS), col),
                  pl.BlockSpec((D_MODEL, UP_COLS), col),
                  pl.BlockSpec((CONV_WIDTH, UP_COLS), col),
                  pl.BlockSpec((CONV_WIDTH, UP_COLS), col),
                  pl.BlockSpec((1, UP_COLS), col),
                  pl.BlockSpec((1, UP_COLS), col)],
        out_specs=pl.BlockSpec((UP_ROWS, UP_COLS), lambda i, j: (i, j)),
        out_shape=jax.ShapeDtypeStruct((seq, FF_PAD), jnp.bfloat16),
        compiler_params=_params("arbitrary", "arbitrary"),
        name="ffn_up",
    )(xb, xb, wg, wv, cwg, cwv, cbg, cbv)


def _ffn_down_ln_kernel(a_ref, x_ref, w_ref, g_ref, b_ref, of_ref, ob_ref, acc_ref):
    k = pl.program_id(1)
    part = jnp.dot(a_ref[...], w_ref[...], preferred_element_type=jnp.float32)

    @pl.when(k == 0)
    def _():
        acc_ref[...] = part

    @pl.when(k > 0)
    def _():
        acc_ref[...] += part

    @pl.when(k == pl.num_programs(1) - 1)
    def _():
        out = _residual_layer_norm(x_ref[...], acc_ref[...], g_ref[...], b_ref[...])
        of_ref[...] = out
        ob_ref[...] = out.astype(jnp.bfloat16)


def _ffn_down_ln(act, x, w, g, b):
    seq = x.shape[0]
    row = lambda i, k: (i, 0)
    fixed = lambda i, k: (0, 0)
    return pl.pallas_call(
        _ffn_down_ln_kernel,
        grid=(seq // DOWN_ROWS, FF_PAD // DOWN_K),
        in_specs=[pl.BlockSpec((DOWN_ROWS, DOWN_K), lambda i, k: (i, k)),
                  pl.BlockSpec((DOWN_ROWS, D_MODEL), row),
                  pl.BlockSpec((DOWN_K, D_MODEL), lambda i, k: (k, 0)),
                  pl.BlockSpec((1, D_MODEL), fixed),
                  pl.BlockSpec((1, D_MODEL), fixed)],
        out_specs=[pl.BlockSpec((DOWN_ROWS, D_MODEL), row),
                   pl.BlockSpec((DOWN_ROWS, D_MODEL), row)],
        out_shape=[jax.ShapeDtypeStruct((seq, D_MODEL), jnp.float32),
                   jax.ShapeDtypeStruct((seq, D_MODEL), jnp.bfloat16)],
        scratch_shapes=[pltpu.VMEM((DOWN_ROWS, D_MODEL), jnp.float32)],
        compiler_params=_params("arbitrary", "arbitrary"),
        name="ffn_down_ln",
    )(act, x, w, g, b)


def _pad_ff(a, axis):
    pad = [(0, 0)] * a.ndim
    pad[axis] = (0, FF_PAD - D_FF)
    return jnp.pad(a, pad)


def kernel(x, pool_w, pool_scale, attn_w_qkv, attn_w_o, ffn_w_up, ffn_conv_w, ffn_conv_b,
           ffn_w_down, ln_mix_g, ln_mix_b, ln_ffn_g, ln_ffn_b):
    batch, seq, _ = x.shape
    bf16 = jnp.bfloat16
    vec = lambda a: a.reshape(1, -1)
    ntri = -jnp.tril(jnp.ones((ATTN_BLOCK, ATTN_BLOCK), bf16), k=-1)

    outs = []
    for bi in range(batch):
        xf = x[bi]
        xb = None
        for i in range(DEPTH):
            j = i // 2
            g, b = vec(ln_mix_g[i]), vec(ln_mix_b[i])
            if i % 2 == 0:
                xf, xb = _pool_ln(xf, pool_w[j].astype(bf16), vec(pool_scale[j]), g, b)
            else:
                qkv = _qkv(xb, attn_w_qkv[j].astype(bf16))
                o = _attention(qkv, ntri)
                xf, xb = _proj_ln(o, xf, attn_w_o[j].astype(bf16), g, b)
            w_up = ffn_w_up[i]
            cw, cb = ffn_conv_w[i], ffn_conv_b[i]
            act = _ffn_up(xb,
                          _pad_ff(w_up[:, :D_FF].astype(bf16), 1),
                          _pad_ff(w_up[:, D_FF:].astype(bf16), 1),
                          _pad_ff(cw[:, :D_FF], 1), _pad_ff(cw[:, D_FF:], 1),
                          _pad_ff(vec(cb[:D_FF]), 1), _pad_ff(vec(cb[D_FF:]), 1))
            xf, xb = _ffn_down_ln(act, xf, _pad_ff(ffn_w_down[i].astype(bf16), 0),
                                  vec(ln_ffn_g[i]), vec(ln_ffn_b[i]))
        outs.append(xf)
    return jnp.stack(outs, axis=0)
```

```python
import functools
import math

import jax
import jax.numpy as jnp
from jax import lax
from jax.experimental import pallas as pl
from jax.experimental.pallas import tpu as pltpu

D_MODEL = 2048
DEPTH = 4
POOL_WINDOWS = (2, 4, 8, 16)
N_POOL_GROUPS = len(POOL_WINDOWS)
POOL_GROUP_DIM = D_MODEL // N_POOL_GROUPS
HEAD_DIM = 128
N_HEADS = D_MODEL // HEAD_DIM
D_FF = 5504
CONV_WIDTH = 3
LN_EPS = 1e-5
DEEPNORM_ALPHA = (2.0 * DEPTH) ** 0.25

V7X_MXU_DIM = 256
V7X_BF16_SUBLANES = 16
V7X_LANES = 128
V7X_VMEM_LIMIT_BYTES = 56 * 1024 * 1024

HALO_ROWS = V7X_BF16_SUBLANES
POOL_ROWS = 512
QKV_ROWS, QKV_COLS = 1024, 512
ATTN_BLOCK = V7X_MXU_DIM
ATTN_HEADS = 4
PROJ_ROWS = 256
FF_PAD = 22 * V7X_MXU_DIM
UP_ROWS, UP_COLS = 1024, 512
UP_ROW_CHUNKS = 1
UP_VAL_BLOCKS = UP_COLS // V7X_LANES
DOWN_ROWS, DOWN_K = 512, FF_PAD // 4

LOG2_E = math.log2(math.e)


def _params(*semantics):
    return pltpu.CompilerParams(dimension_semantics=semantics,
                                vmem_limit_bytes=V7X_VMEM_LIMIT_BYTES)


def _residual_layer_norm(x, y, g, b):
    v = DEEPNORM_ALPHA * x + y
    mu = jnp.mean(v, axis=-1, keepdims=True)
    c = v - mu
    var = jnp.mean(c * c, axis=-1, keepdims=True)
    return c * lax.rsqrt(var + LN_EPS) * g + b


def _prev_halo_index(rows):
    per_tile = rows // HALO_ROWS
    return lambda i, *_: (jnp.maximum(i * per_tile - 1, 0), 0)


def _pool_ln_kernel(x_ref, halo_ref, w_ref, scale_ref, g_ref, b_ref, of_ref, ob_ref):
    i = pl.program_id(0)
    rows = x_ref.shape[0]
    x = x_ref[...]
    halo = jnp.where(i == 0, 0.0, halo_ref[...])
    t1 = i * rows + lax.broadcasted_iota(jnp.int32, (rows, 1), 0) + 1
    ys = []
    for grp, win in enumerate(POOL_WINDOWS):
        cols = slice(grp * POOL_GROUP_DIM, (grp + 1) * POOL_GROUP_DIM)
        xg = x[:, cols]
        s = jnp.concatenate([halo[:, cols], xg], axis=0)
        span = 1
        while span < win:
            s = s + pltpu.roll(s, span, axis=0)
            span *= 2
        cnt = jnp.minimum(t1, win).astype(jnp.float32)
        pooled = s[HALO_ROWS:, :] / cnt - xg
        ys.append(jnp.dot(pooled.astype(jnp.bfloat16), w_ref[grp],
                          preferred_element_type=jnp.float32))
    y = jnp.concatenate(ys, axis=1) * scale_ref[...]
    out = _residual_layer_norm(x, y, g_ref[...], b_ref[...])
    of_ref[...] = out
    ob_ref[...] = out.astype(jnp.bfloat16)


def _pool_ln(x, w, layer, scale, g, b):
    seq = x.shape[0]
    row = lambda i: (i, 0)
    fixed = lambda i: (0, 0)
    return pl.pallas_call(
        _pool_ln_kernel,
        grid=(seq // POOL_ROWS,),
        in_specs=[pl.BlockSpec((POOL_ROWS, D_MODEL), row),
                  pl.BlockSpec((HALO_ROWS, D_MODEL), _prev_halo_index(POOL_ROWS)),
                  pl.BlockSpec((None,) + w.shape[1:], lambda i: (layer, 0, 0, 0)),
                  pl.BlockSpec((1, D_MODEL), fixed),
                  pl.BlockSpec((1, D_MODEL), fixed),
                  pl.BlockSpec((1, D_MODEL), fixed)],
        out_specs=[pl.BlockSpec((POOL_ROWS, D_MODEL), row),
                   pl.BlockSpec((POOL_ROWS, D_MODEL), row)],
        out_shape=[jax.ShapeDtypeStruct((seq, D_MODEL), jnp.float32),
                   jax.ShapeDtypeStruct((seq, D_MODEL), jnp.bfloat16)],
        compiler_params=_params("arbitrary"),
        name="pool_ln",
    )(x, x, w, scale, g, b)


def _qkv_kernel(x_ref, w_ref, o_ref, *, q_col_tiles, q_scale):
    j = pl.program_id(1)
    h = jnp.dot(x_ref[...], w_ref[...], preferred_element_type=jnp.float32)
    h = h * jnp.where(j < q_col_tiles, q_scale, 1.0)
    o_ref[...] = h.astype(jnp.bfloat16)


def _qkv(xb, w, layer):
    seq = xb.shape[0]
    n_out = w.shape[2]
    kern = functools.partial(_qkv_kernel, q_col_tiles=D_MODEL // QKV_COLS,
                             q_scale=HEAD_DIM ** -0.5 * LOG2_E)
    return pl.pallas_call(
        kern,
        grid=(seq // QKV_ROWS, n_out // QKV_COLS),
        in_specs=[pl.BlockSpec((QKV_ROWS, D_MODEL), lambda i, j: (i, 0)),
                  pl.BlockSpec((None, D_MODEL, QKV_COLS), lambda i, j: (layer, 0, j))],
        out_specs=pl.BlockSpec((QKV_ROWS, QKV_COLS), lambda i, j: (i, j)),
        out_shape=jax.ShapeDtypeStruct((seq, n_out), jnp.bfloat16),
        compiler_params=_params("arbitrary", "arbitrary"),
        name="qkv",
    )(xb, w)


def _neg_abs(z):
    bits = lax.bitcast_convert_type(z, jnp.uint32) | jnp.uint32(0x80000000)
    return lax.bitcast_convert_type(bits, jnp.float32)


def _attn_kernel(q_ref, k_ref, v_ref, ntri_ref, o_ref, z_ref, a_ref, acc_ref, carry_ref):
    i = pl.program_id(1)
    blk = q_ref.shape[0]
    n_heads = q_ref.shape[1] // HEAD_DIM
    ntri = ntri_ref[...]
    head_lanes = [slice(hd * HEAD_DIM, (hd + 1) * HEAD_DIM) for hd in range(n_heads)]

    def logits(hd, kb):
        start = pl.multiple_of(kb * blk, blk)
        return lax.dot_general(q_ref[:, head_lanes[hd]], k_ref[pl.ds(start, blk), head_lanes[hd]],
                               (((1,), (1,)), ((), ())),
                               preferred_element_type=jnp.float32)

    def weigh(hd, kb):
        start = pl.multiple_of(kb * blk, blk)
        acc_ref[hd] += jnp.dot(a_ref[hd], v_ref[pl.ds(start, blk), head_lanes[hd]],
                               preferred_element_type=jnp.float32)

    def shape(hd, slot):
        z = z_ref[slot, hd]
        p = jnp.maximum(z, 0.0) + jnp.log2(1.0 + jnp.exp2(_neg_abs(z)))
        pb = p.astype(jnp.bfloat16)
        cs = jnp.dot(pb, ntri, preferred_element_type=jnp.float32)
        carry = carry_ref[hd]
        a_ref[hd] = jnp.exp2(z - p + cs + carry).astype(jnp.bfloat16)
        carry_ref[hd] = carry + cs[:, 0:1] - pb[:, 0:1].astype(jnp.float32)

    r = lax.broadcasted_iota(jnp.int32, (blk, blk), 0)
    c = lax.broadcasted_iota(jnp.int32, (blk, blk), 1)
    for hd in range(n_heads):
        z_ref[0, hd] = jnp.where(c < r, logits(hd, i), -jnp.inf)
        a_ref[hd] = jnp.zeros((blk, blk), jnp.bfloat16)
        acc_ref[hd] = jnp.zeros((blk, HEAD_DIM), jnp.float32)
        carry_ref[hd] = jnp.zeros((blk, 1), jnp.float32)

    def trip(t, slot):
        for hd in range(n_heads):
            weigh(hd, jnp.minimum(i - t + 1, i))
        for hd in range(n_heads):
            shape(hd, slot)
        for hd in range(n_heads):
            z_ref[1 - slot, hd] = logits(hd, jnp.maximum(i - t - 1, 0))

    def trip_pair(n, _):
        trip(2 * n, 0)
        trip(2 * n + 1, 1)
        return 0

    n_trips = i + 1
    lax.fori_loop(0, n_trips // 2, trip_pair, 0)

    @pl.when(n_trips % 2 == 1)
    def _():
        trip(i, 0)

    for hd in range(n_heads):
        weigh(hd, 0)
        o_ref[:, head_lanes[hd]] = acc_ref[hd].astype(jnp.bfloat16)


def _attention(qkv, ntri):
    seq = qkv.shape[0]
    width = ATTN_HEADS * HEAD_DIM
    groups = N_HEADS // ATTN_HEADS
    return pl.pallas_call(
        _attn_kernel,
        grid=(groups, seq // ATTN_BLOCK),
        in_specs=[pl.BlockSpec((ATTN_BLOCK, width), lambda h, i: (i, h)),
                  pl.BlockSpec((seq, width), lambda h, i: (0, groups + h)),
                  pl.BlockSpec((seq, width), lambda h, i: (0, 2 * groups + h)),
                  pl.BlockSpec((ATTN_BLOCK, ATTN_BLOCK), lambda h, i: (0, 0))],
        out_specs=pl.BlockSpec((ATTN_BLOCK, width), lambda h, i: (i, h)),
        out_shape=jax.ShapeDtypeStruct((seq, D_MODEL), jnp.bfloat16),
        scratch_shapes=[pltpu.VMEM((2, ATTN_HEADS, ATTN_BLOCK, ATTN_BLOCK), jnp.float32),
                        pltpu.VMEM((ATTN_HEADS, ATTN_BLOCK, ATTN_BLOCK), jnp.bfloat16),
                        pltpu.VMEM((ATTN_HEADS, ATTN_BLOCK, HEAD_DIM), jnp.float32),
                        pltpu.VMEM((ATTN_HEADS, ATTN_BLOCK, 1), jnp.float32)],
        compiler_params=_params("arbitrary", "arbitrary"),
        name="stickbreak_attn",
    )(qkv, qkv, qkv, ntri)


def _proj_ln_kernel(o_ref, x_ref, w_ref, g_ref, b_ref, of_ref, ob_ref):
    y = jnp.dot(o_ref[...], w_ref[...], preferred_element_type=jnp.float32)
    out = _residual_layer_norm(x_ref[...], y, g_ref[...], b_ref[...])
    of_ref[...] = out
    ob_ref[...] = out.astype(jnp.bfloat16)


def _proj_ln(o, x, w, layer, g, b):
    seq = x.shape[0]
    row = lambda i: (i, 0)
    fixed = lambda i: (0, 0)
    return pl.pallas_call(
        _proj_ln_kernel,
        grid=(seq // PROJ_ROWS,),
        in_specs=[pl.BlockSpec((PROJ_ROWS, D_MODEL), row),
                  pl.BlockSpec((PROJ_ROWS, D_MODEL), row),
                  pl.BlockSpec((None, D_MODEL, D_MODEL), lambda i: (layer, 0, 0)),
                  pl.BlockSpec((1, D_MODEL), fixed),
                  pl.BlockSpec((1, D_MODEL), fixed)],
        out_specs=[pl.BlockSpec((PROJ_ROWS, D_MODEL), row),
                   pl.BlockSpec((PROJ_ROWS, D_MODEL), row)],
        out_shape=[jax.ShapeDtypeStruct((seq, D_MODEL), jnp.float32),
                   jax.ShapeDtypeStruct((seq, D_MODEL), jnp.bfloat16)],
        compiler_params=_params("arbitrary"),
        name="proj_ln",
    )(o, x, w, g, b)


def _causal_taps(h, halo):
    rows8 = lax.broadcasted_iota(jnp.int32, (8, 1), 0)
    top1 = pltpu.roll(halo, 1, axis=0)[0:8]
    top2 = pltpu.roll(halo, 2, axis=0)[0:8]
    r1 = pltpu.roll(h, 1, axis=0)
    r2 = pltpu.roll(h, 2, axis=0)
    h1 = jnp.concatenate([jnp.where(rows8 < 1, top1, r1[0:8]), r1[8:]], axis=0)
    h2 = jnp.concatenate([jnp.where(rows8 < 2, top2, r2[0:8]), r2[8:]], axis=0)
    return h1, h2


def _ffn_up_kernel(x_ref, halo_ref, wg_ref, *rest):
    wv_refs = rest[:UP_VAL_BLOCKS]
    cwg_ref, cwv_ref, cbg_ref, cbv_ref, o_ref, xe_ref = rest[UP_VAL_BLOCKS:]
    i = pl.program_id(0)

    @pl.when(pl.program_id(1) == 0)
    def _():
        xe_ref[0:HALO_ROWS, :] = jnp.where(i == 0, jnp.zeros_like(halo_ref), halo_ref[...])
        xe_ref[HALO_ROWS:, :] = x_ref[...]

    def conv_branch(xe, w, cw_ref, cb_ref):
        he = jnp.dot(xe, w, preferred_element_type=jnp.float32)
        h = he[HALO_ROWS:]
        h1, h2 = _causal_taps(h, he[0:HALO_ROWS])
        cw = cw_ref[...]
        return cb_ref[...] + cw[0:1] * h2 + cw[1:2] * h1 + cw[2:3] * h

    chunk = o_ref.shape[0] // UP_ROW_CHUNKS
    wg = wg_ref[...]
    wv = jnp.concatenate([r[...] for r in wv_refs], axis=1)
    for r in range(UP_ROW_CHUNKS):
        xe = xe_ref[r * chunk:(r + 1) * chunk + HALO_ROWS, :]
        gate = conv_branch(xe, wg, cwg_ref, cbg_ref)
        val = conv_branch(xe, wv, cwv_ref, cbv_ref)
        o_ref[r * chunk:(r + 1) * chunk, :] = (gate * jax.nn.sigmoid(gate) * val).astype(jnp.bfloat16)


def _ffn_up(xb, w_up, layer, cwg, cwv, cbg, cbv):
    seq = xb.shape[0]
    col = lambda i, j: (0, j)
    val0 = D_FF // V7X_LANES
    last = 2 * D_FF // V7X_LANES - 1

    def val_index(m, i, j):
        return layer, 0, jnp.minimum(val0 + UP_VAL_BLOCKS * j + m, last)

    val_specs = [pl.BlockSpec((None, D_MODEL, V7X_LANES), functools.partial(val_index, m))
                 for m in range(UP_VAL_BLOCKS)]
    return pl.pallas_call(
        _ffn_up_kernel,
        grid=(seq // UP_ROWS, FF_PAD // UP_COLS),
        in_specs=[pl.BlockSpec((UP_ROWS, D_MODEL), lambda i, j: (i, 0)),
                  pl.BlockSpec((HALO_ROWS, D_MODEL), _prev_halo_index(UP_ROWS)),
                  pl.BlockSpec((None, D_MODEL, UP_COLS), lambda i, j: (layer, 0, j)),
                  *val_specs,
                  pl.BlockSpec((CONV_WIDTH, UP_COLS), col),
                  pl.BlockSpec((CONV_WIDTH, UP_COLS), col),
                  pl.BlockSpec((1, UP_COLS), col),
                  pl.BlockSpec((1, UP_COLS), col)],
        out_specs=pl.BlockSpec((UP_ROWS, UP_COLS), lambda i, j: (i, j)),
        out_shape=jax.ShapeDtypeStruct((seq, FF_PAD), jnp.bfloat16),
        scratch_shapes=[pltpu.VMEM((HALO_ROWS + UP_ROWS, D_MODEL), jnp.bfloat16)],
        compiler_params=_params("arbitrary", "arbitrary"),
        name="ffn_up",
    )(xb, xb, w_up, *([w_up] * UP_VAL_BLOCKS), cwg, cwv, cbg, cbv)


def _ffn_down_ln_kernel(a_ref, x_ref, w_ref, g_ref, b_ref, of_ref, ob_ref, acc_ref):
    k = pl.program_id(1)

    @pl.when(k == 0)
    def _():
        acc_ref[...] = jnp.zeros_like(acc_ref)

    rows = k * w_ref.shape[0] + lax.broadcasted_iota(jnp.int32, (w_ref.shape[0], 1), 0)
    w = jnp.where(rows < D_FF, w_ref[...], jnp.zeros_like(w_ref))
    acc_ref[...] += jnp.dot(a_ref[...], w, preferred_element_type=jnp.float32)

    @pl.when(k == pl.num_programs(1) - 1)
    def _():
        out = _residual_layer_norm(x_ref[...], acc_ref[...], g_ref[...], b_ref[...])
        of_ref[...] = out
        ob_ref[...] = out.astype(jnp.bfloat16)


def _ffn_down_ln(act, x, w, layer, g, b):
    seq = x.shape[0]
    row = lambda i, k: (i, 0)
    fixed = lambda i, k: (0, 0)
    return pl.pallas_call(
        _ffn_down_ln_kernel,
        grid=(seq // DOWN_ROWS, FF_PAD // DOWN_K),
        in_specs=[pl.BlockSpec((DOWN_ROWS, DOWN_K), lambda i, k: (i, k)),
                  pl.BlockSpec((DOWN_ROWS, D_MODEL), row),
                  pl.BlockSpec((None, DOWN_K, D_MODEL), lambda i, k: (layer, k, 0)),
                  pl.BlockSpec((1, D_MODEL), fixed),
                  pl.BlockSpec((1, D_MODEL), fixed)],
        out_specs=[pl.BlockSpec((DOWN_ROWS, D_MODEL), row),
                   pl.BlockSpec((DOWN_ROWS, D_MODEL), row)],
        out_shape=[jax.ShapeDtypeStruct((seq, D_MODEL), jnp.float32),
                   jax.ShapeDtypeStruct((seq, D_MODEL), jnp.bfloat16)],
        scratch_shapes=[pltpu.VMEM((DOWN_ROWS, D_MODEL), jnp.float32)],
        compiler_params=_params("arbitrary", "arbitrary"),
        name="ffn_down_ln",
    )(act, x, w, g, b)


def _pad_ff(a, axis):
    pad = [(0, 0)] * a.ndim
    pad[axis] = (0, FF_PAD - D_FF)
    return jnp.pad(a, pad)


def kernel(x, pool_w, pool_scale, attn_w_qkv, attn_w_o, ffn_w_up, ffn_conv_w, ffn_conv_b,
           ffn_w_down, ln_mix_g, ln_mix_b, ln_ffn_g, ln_ffn_b):
    batch, seq, _ = x.shape
    bf16 = jnp.bfloat16
    vec = lambda a: a.reshape(1, -1)
    ntri = -jnp.tril(jnp.ones((ATTN_BLOCK, ATTN_BLOCK), bf16), k=-1)
    pool_wb, w_qkv, w_o = pool_w.astype(bf16), attn_w_qkv.astype(bf16), attn_w_o.astype(bf16)
    w_up, w_down = ffn_w_up.astype(bf16), ffn_w_down.astype(bf16)

    outs = []
    for bi in range(batch):
        xf = x[bi]
        xb = None
        for i in range(DEPTH):
            j = i // 2
            g, b = vec(ln_mix_g[i]), vec(ln_mix_b[i])
            if i % 2 == 0:
                xf, xb = _pool_ln(xf, pool_wb, j, vec(pool_scale[j]), g, b)
            else:
                qkv = _qkv(xb, w_qkv, j)
                o = _attention(qkv, ntri)
                xf, xb = _proj_ln(o, xf, w_o, j, g, b)
            cw, cb = ffn_conv_w[i], ffn_conv_b[i]
            act = _ffn_up(xb, w_up, i,
                          _pad_ff(cw[:, :D_FF], 1), _pad_ff(cw[:, D_FF:], 1),
                          _pad_ff(vec(cb[:D_FF]), 1), _pad_ff(vec(cb[D_FF:]), 1))
            xf, xb = _ffn_down_ln(act, xf, w_down, i, vec(ln_ffn_g[i]), vec(ln_ffn_b[i]))
        outs.append(xf)
    return jnp.stack(outs, axis=0)
```

```python
import functools
import math

import jax
import jax.numpy as jnp
from jax import lax
from jax.experimental import pallas as pl
from jax.experimental.pallas import tpu as pltpu

D_MODEL = 2048
DEPTH = 4
POOL_WINDOWS = (2, 4, 8, 16)
N_POOL_GROUPS = len(POOL_WINDOWS)
POOL_GROUP_DIM = D_MODEL // N_POOL_GROUPS
HEAD_DIM = 128
N_HEADS = D_MODEL // HEAD_DIM
D_FF = 5504
CONV_WIDTH = 3
LN_EPS = 1e-5
DEEPNORM_ALPHA = (2.0 * DEPTH) ** 0.25

V7X_MXU_DIM = 256
V7X_BF16_SUBLANES = 16
V7X_LANES = 128
V7X_VMEM_LIMIT_BYTES = 56 * 1024 * 1024

HALO_ROWS = V7X_BF16_SUBLANES
POOL_ROWS = 512
QKV_ROWS, QKV_COLS = 1024, 512
ATTN_BLOCK = V7X_MXU_DIM
ATTN_HEADS = 4
ATTN_UNROLL = 4
PROJ_ROWS = 256
FF_PAD = 22 * V7X_MXU_DIM
UP_ROWS, UP_COLS = 1024, 512
UP_ROW_CHUNKS = 1
UP_VAL_BLOCKS = UP_COLS // V7X_LANES
DOWN_ROWS, DOWN_K = 512, FF_PAD // 4

LOG2_E = math.log2(math.e)


def _params(*semantics):
    return pltpu.CompilerParams(dimension_semantics=semantics,
                                vmem_limit_bytes=V7X_VMEM_LIMIT_BYTES)


def _residual_layer_norm(x, y, g, b):
    v = DEEPNORM_ALPHA * x + y
    mu = jnp.mean(v, axis=-1, keepdims=True)
    c = v - mu
    var = jnp.mean(c * c, axis=-1, keepdims=True)
    return c * lax.rsqrt(var + LN_EPS) * g + b


def _prev_halo_index(rows):
    per_tile = rows // HALO_ROWS
    return lambda i, *_: (jnp.maximum(i * per_tile - 1, 0), 0)


def _pool_ln_kernel(x_ref, halo_ref, w_ref, scale_ref, g_ref, b_ref, of_ref, ob_ref):
    i = pl.program_id(0)
    rows = x_ref.shape[0]
    x = x_ref[...]
    halo = jnp.where(i == 0, 0.0, halo_ref[...])
    t1 = i * rows + lax.broadcasted_iota(jnp.int32, (rows, 1), 0) + 1
    ys = []
    for grp, win in enumerate(POOL_WINDOWS):
        cols = slice(grp * POOL_GROUP_DIM, (grp + 1) * POOL_GROUP_DIM)
        xg = x[:, cols]
        s = jnp.concatenate([halo[:, cols], xg], axis=0)
        span = 1
        while span < win:
            s = s + pltpu.roll(s, span, axis=0)
            span *= 2
        cnt = jnp.minimum(t1, win).astype(jnp.float32)
        pooled = s[HALO_ROWS:, :] / cnt - xg
        ys.append(jnp.dot(pooled.astype(jnp.bfloat16), w_ref[grp],
                          preferred_element_type=jnp.float32))
    y = jnp.concatenate(ys, axis=1) * scale_ref[...]
    out = _residual_layer_norm(x, y, g_ref[...], b_ref[...])
    of_ref[...] = out
    ob_ref[...] = out.astype(jnp.bfloat16)


def _pool_ln(x, w, layer, scale, g, b):
    seq = x.shape[0]
    row = lambda i: (i, 0)
    fixed = lambda i: (0, 0)
    return pl.pallas_call(
        _pool_ln_kernel,
        grid=(seq // POOL_ROWS,),
        in_specs=[pl.BlockSpec((POOL_ROWS, D_MODEL), row),
                  pl.BlockSpec((HALO_ROWS, D_MODEL), _prev_halo_index(POOL_ROWS)),
                  pl.BlockSpec((None,) + w.shape[1:], lambda i: (layer, 0, 0, 0)),
                  pl.BlockSpec((1, D_MODEL), fixed),
                  pl.BlockSpec((1, D_MODEL), fixed),
                  pl.BlockSpec((1, D_MODEL), fixed)],
        out_specs=[pl.BlockSpec((POOL_ROWS, D_MODEL), row),
                   pl.BlockSpec((POOL_ROWS, D_MODEL), row)],
        out_shape=[jax.ShapeDtypeStruct((seq, D_MODEL), jnp.float32),
                   jax.ShapeDtypeStruct((seq, D_MODEL), jnp.bfloat16)],
        compiler_params=_params("arbitrary"),
        name="pool_ln",
    )(x, x, w, scale, g, b)


def _qkv_kernel(x_ref, w_ref, o_ref, *, q_col_tiles, q_scale):
    j = pl.program_id(1)
    h = jnp.dot(x_ref[...], w_ref[...], preferred_element_type=jnp.float32)
    h = h * jnp.where(j < q_col_tiles, q_scale, 1.0)
    o_ref[...] = h.astype(jnp.bfloat16)


def _qkv(xb, w, layer):
    seq = xb.shape[0]
    n_out = w.shape[2]
    kern = functools.partial(_qkv_kernel, q_col_tiles=D_MODEL // QKV_COLS,
                             q_scale=HEAD_DIM ** -0.5 * LOG2_E)
    return pl.pallas_call(
        kern,
        grid=(seq // QKV_ROWS, n_out // QKV_COLS),
        in_specs=[pl.BlockSpec((QKV_ROWS, D_MODEL), lambda i, j: (i, 0)),
                  pl.BlockSpec((None, D_MODEL, QKV_COLS), lambda i, j: (layer, 0, j))],
        out_specs=pl.BlockSpec((QKV_ROWS, QKV_COLS), lambda i, j: (i, j)),
        out_shape=jax.ShapeDtypeStruct((seq, n_out), jnp.bfloat16),
        compiler_params=_params("arbitrary", "arbitrary"),
        name="qkv",
    )(xb, w)


def _neg_abs(z):
    bits = lax.bitcast_convert_type(z, jnp.uint32) | jnp.uint32(0x80000000)
    return lax.bitcast_convert_type(bits, jnp.float32)


def _attn_kernel(q_ref, k_ref, v_ref, ntri_ref, o_ref, z_ref, a_ref, acc_ref, carry_ref):
    i = pl.program_id(1)
    blk = q_ref.shape[0]
    n_heads = q_ref.shape[1] // HEAD_DIM
    ntri = ntri_ref[...]
    head_lanes = [slice(hd * HEAD_DIM, (hd + 1) * HEAD_DIM) for hd in range(n_heads)]

    def logits(hd, kb):
        start = pl.multiple_of(kb * blk, blk)
        return lax.dot_general(q_ref[:, head_lanes[hd]], k_ref[pl.ds(start, blk), head_lanes[hd]],
                               (((1,), (1,)), ((), ())),
                               preferred_element_type=jnp.float32)

    def weigh(hd, kb):
        start = pl.multiple_of(kb * blk, blk)
        acc_ref[hd] += jnp.dot(a_ref[hd], v_ref[pl.ds(start, blk), head_lanes[hd]],
                               preferred_element_type=jnp.float32)

    def shape_all(slot):
        us, pbs = [], []
        for hd in range(n_heads):
            z = z_ref[slot, hd]
            p = jnp.maximum(z, 0.0) + jnp.log2(1.0 + jnp.exp2(_neg_abs(z)))
            us.append(z - p)
            pbs.append(p.astype(jnp.bfloat16))
        cs_all = jnp.dot(jnp.concatenate(pbs, axis=0), ntri,
                         preferred_element_type=jnp.float32)
        for hd in range(n_heads):
            cs = cs_all[hd * blk:(hd + 1) * blk]
            carry = carry_ref[hd]
            a_ref[hd] = jnp.exp2(us[hd] + cs + carry).astype(jnp.bfloat16)
            carry_ref[hd] = carry + cs[:, 0:1] - pbs[hd][:, 0:1].astype(jnp.float32)

    r = lax.broadcasted_iota(jnp.int32, (blk, blk), 0)
    c = lax.broadcasted_iota(jnp.int32, (blk, blk), 1)
    for hd in range(n_heads):
        z_ref[0, hd] = jnp.where(c < r, logits(hd, i), -jnp.inf)
        a_ref[hd] = jnp.zeros((blk, blk), jnp.bfloat16)
        acc_ref[hd] = jnp.zeros((blk, HEAD_DIM), jnp.float32)
        carry_ref[hd] = jnp.zeros((blk, 1), jnp.float32)

    def trip(t, slot):
        for hd in range(n_heads):
            weigh(hd, jnp.minimum(i - t + 1, i))
        shape_all(slot)
        for hd in range(n_heads):
            z_ref[1 - slot, hd] = logits(hd, jnp.maximum(i - t - 1, 0))

    def trips(first, count):
        for n in range(count):
            trip(first + n, n % 2)

    n_trips = i + 1
    lax.fori_loop(0, n_trips // ATTN_UNROLL,
                  lambda n, _: trips(ATTN_UNROLL * n, ATTN_UNROLL) or 0, 0)
    tail = n_trips % ATTN_UNROLL
    done = n_trips - tail
    for pairs in range(1, ATTN_UNROLL // 2):
        @pl.when(tail // 2 == pairs)
        def _():
            trips(done, 2 * pairs)

    @pl.when(tail % 2 == 1)
    def _():
        trip(i, 0)

    for hd in range(n_heads):
        weigh(hd, 0)
        o_ref[:, head_lanes[hd]] = acc_ref[hd].astype(jnp.bfloat16)


def _attention(qkv, ntri):
    seq = qkv.shape[0]
    width = ATTN_HEADS * HEAD_DIM
    groups = N_HEADS // ATTN_HEADS
    return pl.pallas_call(
        _attn_kernel,
        grid=(groups, seq // ATTN_BLOCK),
        in_specs=[pl.BlockSpec((ATTN_BLOCK, width), lambda h, i: (i, h)),
                  pl.BlockSpec((seq, width), lambda h, i: (0, groups + h)),
                  pl.BlockSpec((seq, width), lambda h, i: (0, 2 * groups + h)),
                  pl.BlockSpec((ATTN_BLOCK, ATTN_BLOCK), lambda h, i: (0, 0))],
        out_specs=pl.BlockSpec((ATTN_BLOCK, width), lambda h, i: (i, h)),
        out_shape=jax.ShapeDtypeStruct((seq, D_MODEL), jnp.bfloat16),
        scratch_shapes=[pltpu.VMEM((2, ATTN_HEADS, ATTN_BLOCK, ATTN_BLOCK), jnp.float32),
                        pltpu.VMEM((ATTN_HEADS, ATTN_BLOCK, ATTN_BLOCK), jnp.bfloat16),
                        pltpu.VMEM((ATTN_HEADS, ATTN_BLOCK, HEAD_DIM), jnp.float32),
                        pltpu.VMEM((ATTN_HEADS, ATTN_BLOCK, 1), jnp.float32)],
        compiler_params=_params("arbitrary", "arbitrary"),
        name="stickbreak_attn",
    )(qkv, qkv, qkv, ntri)


def _proj_ln_kernel(o_ref, x_ref, w_ref, g_ref, b_ref, of_ref, ob_ref):
    y = jnp.dot(o_ref[...], w_ref[...], preferred_element_type=jnp.float32)
    out = _residual_layer_norm(x_ref[...], y, g_ref[...], b_ref[...])
    of_ref[...] = out
    ob_ref[...] = out.astype(jnp.bfloat16)


def _proj_ln(o, x, w, layer, g, b):
    seq = x.shape[0]
    row = lambda i: (i, 0)
    fixed = lambda i: (0, 0)
    return pl.pallas_call(
        _proj_ln_kernel,
        grid=(seq // PROJ_ROWS,),
        in_specs=[pl.BlockSpec((PROJ_ROWS, D_MODEL), row),
                  pl.BlockSpec((PROJ_ROWS, D_MODEL), row),
                  pl.BlockSpec((None, D_MODEL, D_MODEL), lambda i: (layer, 0, 0)),
                  pl.BlockSpec((1, D_MODEL), fixed),
                  pl.BlockSpec((1, D_MODEL), fixed)],
        out_specs=[pl.BlockSpec((PROJ_ROWS, D_MODEL), row),
                   pl.BlockSpec((PROJ_ROWS, D_MODEL), row)],
        out_shape=[jax.ShapeDtypeStruct((seq, D_MODEL), jnp.float32),
                   jax.ShapeDtypeStruct((seq, D_MODEL), jnp.bfloat16)],
        compiler_params=_params("arbitrary"),
        name="proj_ln",
    )(o, x, w, g, b)


def _causal_taps(h, halo):
    rows8 = lax.broadcasted_iota(jnp.int32, (8, 1), 0)
    top1 = pltpu.roll(halo, 1, axis=0)[0:8]
    top2 = pltpu.roll(halo, 2, axis=0)[0:8]
    r1 = pltpu.roll(h, 1, axis=0)
    r2 = pltpu.roll(h, 2, axis=0)
    h1 = jnp.concatenate([jnp.where(rows8 < 1, top1, r1[0:8]), r1[8:]], axis=0)
    h2 = jnp.concatenate([jnp.where(rows8 < 2, top2, r2[0:8]), r2[8:]], axis=0)
    return h1, h2


def _ffn_up_kernel(x_ref, halo_ref, wg_ref, *rest):
    wv_refs = rest[:UP_VAL_BLOCKS]
    cwg_ref, cwv_ref, cbg_ref, cbv_ref, o_ref, xe_ref = rest[UP_VAL_BLOCKS:]
    i = pl.program_id(0)

    @pl.when(pl.program_id(1) == 0)
    def _():
        xe_ref[0:HALO_ROWS, :] = jnp.where(i == 0, jnp.zeros_like(halo_ref), halo_ref[...])
        xe_ref[HALO_ROWS:, :] = x_ref[...]

    def conv_branch(xe, w, cw_ref, cb_ref):
        he = jnp.dot(xe, w, preferred_element_type=jnp.float32)
        h = he[HALO_ROWS:]
        h1, h2 = _causal_taps(h, he[0:HALO_ROWS])
        cw = cw_ref[...]
        return cb_ref[...] + cw[0:1] * h2 + cw[1:2] * h1 + cw[2:3] * h

    chunk = o_ref.shape[0] // UP_ROW_CHUNKS
    wg = wg_ref[...]
    wv = jnp.concatenate([r[...] for r in wv_refs], axis=1)
    for r in range(UP_ROW_CHUNKS):
        xe = xe_ref[r * chunk:(r + 1) * chunk + HALO_ROWS, :]
        gate = conv_branch(xe, wg, cwg_ref, cbg_ref)
        val = conv_branch(xe, wv, cwv_ref, cbv_ref)
        o_ref[r * chunk:(r + 1) * chunk, :] = (gate * jax.nn.sigmoid(gate) * val).astype(jnp.bfloat16)


def _ffn_up(xb, w_up, layer, cwg, cwv, cbg, cbv):
    seq = xb.shape[0]
    col = lambda i, j: (0, j)
    val0 = D_FF // V7X_LANES
    last = 2 * D_FF // V7X_LANES - 1

    def val_index(m, i, j):
        return layer, 0, jnp.minimum(val0 + UP_VAL_BLOCKS * j + m, last)

    val_specs = [pl.BlockSpec((None, D_MODEL, V7X_LANES), functools.partial(val_index, m))
                 for m in range(UP_VAL_BLOCKS)]
    return pl.pallas_call(
        _ffn_up_kernel,
        grid=(seq // UP_ROWS, FF_PAD // UP_COLS),
        in_specs=[pl.BlockSpec((UP_ROWS, D_MODEL), lambda i, j: (i, 0)),
                  pl.BlockSpec((HALO_ROWS, D_MODEL), _prev_halo_index(UP_ROWS)),
                  pl.BlockSpec((None, D_MODEL, UP_COLS), lambda i, j: (layer, 0, j)),
                  *val_specs,
                  pl.BlockSpec((CONV_WIDTH, UP_COLS), col),
                  pl.BlockSpec((CONV_WIDTH, UP_COLS), col),
                  pl.BlockSpec((1, UP_COLS), col),
                  pl.BlockSpec((1, UP_COLS), col)],
        out_specs=pl.BlockSpec((UP_ROWS, UP_COLS), lambda i, j: (i, j)),
        out_shape=jax.ShapeDtypeStruct((seq, FF_PAD), jnp.bfloat16),
        scratch_shapes=[pltpu.VMEM((HALO_ROWS + UP_ROWS, D_MODEL), jnp.bfloat16)],
        compiler_params=_params("arbitrary", "arbitrary"),
        name="ffn_up",
    )(xb, xb, w_up, *([w_up] * UP_VAL_BLOCKS), cwg, cwv, cbg, cbv)


def _ffn_down_ln_kernel(a_ref, x_ref, w_ref, g_ref, b_ref, of_ref, ob_ref, acc_ref):
    k = pl.program_id(1)

    @pl.when(k == 0)
    def _():
        acc_ref[...] = jnp.zeros_like(acc_ref)

    rows = k * w_ref.shape[0] + lax.broadcasted_iota(jnp.int32, (w_ref.shape[0], 1), 0)
    w = jnp.where(rows < D_FF, w_ref[...], jnp.zeros_like(w_ref))
    acc_ref[...] += jnp.dot(a_ref[...], w, preferred_element_type=jnp.float32)

    @pl.when(k == pl.num_programs(1) - 1)
    def _():
        out = _residual_layer_norm(x_ref[...], acc_ref[...], g_ref[...], b_ref[...])
        of_ref[...] = out
        ob_ref[...] = out.astype(jnp.bfloat16)


def _ffn_down_ln(act, x, w, layer, g, b):
    seq = x.shape[0]
    row = lambda i, k: (i, 0)
    fixed = lambda i, k: (0, 0)
    return pl.pallas_call(
        _ffn_down_ln_kernel,
        grid=(seq // DOWN_ROWS, FF_PAD // DOWN_K),
        in_specs=[pl.BlockSpec((DOWN_ROWS, DOWN_K), lambda i, k: (i, k)),
                  pl.BlockSpec((DOWN_ROWS, D_MODEL), row),
                  pl.BlockSpec((None, DOWN_K, D_MODEL), lambda i, k: (layer, k, 0)),
                  pl.BlockSpec((1, D_MODEL), fixed),
                  pl.BlockSpec((1, D_MODEL), fixed)],
        out_specs=[pl.BlockSpec((DOWN_ROWS, D_MODEL), row),
                   pl.BlockSpec((DOWN_ROWS, D_MODEL), row)],
        out_shape=[jax.ShapeDtypeStruct((seq, D_MODEL), jnp.float32),
                   jax.ShapeDtypeStruct((seq, D_MODEL), jnp.bfloat16)],
        scratch_shapes=[pltpu.VMEM((DOWN_ROWS, D_MODEL), jnp.float32)],
        compiler_params=_params("arbitrary", "arbitrary"),
        name="ffn_down_ln",
    )(act, x, w, g, b)


def _pad_ff(a, axis):
    pad = [(0, 0)] * a.ndim
    pad[axis] = (0, FF_PAD - D_FF)
    return jnp.pad(a, pad)


def kernel(x, pool_w, pool_scale, attn_w_qkv, attn_w_o, ffn_w_up, ffn_conv_w, ffn_conv_b,
           ffn_w_down, ln_mix_g, ln_mix_b, ln_ffn_g, ln_ffn_b):
    batch, seq, _ = x.shape
    bf16 = jnp.bfloat16
    vec = lambda a: a.reshape(1, -1)
    ntri = -jnp.tril(jnp.ones((ATTN_BLOCK, ATTN_BLOCK), bf16), k=-1)
    pool_wb, w_qkv, w_o = pool_w.astype(bf16), attn_w_qkv.astype(bf16), attn_w_o.astype(bf16)
    w_up, w_down = ffn_w_up.astype(bf16), ffn_w_down.astype(bf16)

    outs = []
    for bi in range(batch):
        xf = x[bi]
        xb = None
        for i in range(DEPTH):
            j = i // 2
            g, b = vec(ln_mix_g[i]), vec(ln_mix_b[i])
            if i % 2 == 0:
                xf, xb = _pool_ln(xf, pool_wb, j, vec(pool_scale[j]), g, b)
            else:
                qkv = _qkv(xb, w_qkv, j)
                o = _attention(qkv, ntri)
                xf, xb = _proj_ln(o, xf, w_o, j, g, b)
            cw, cb = ffn_conv_w[i], ffn_conv_b[i]
            act = _ffn_up(xb, w_up, i,
                          _pad_ff(cw[:, :D_FF], 1), _pad_ff(cw[:, D_FF:], 1),
                          _pad_ff(vec(cb[:D_FF]), 1), _pad_ff(vec(cb[D_FF:]), 1))
            xf, xb = _ffn_down_ln(act, xf, w_down, i, vec(ln_ffn_g[i]), vec(ln_ffn_b[i]))
        outs.append(xf)
    return jnp.stack(outs, axis=0)
```

```python
import functools
import math

import jax
import jax.numpy as jnp
from jax import lax
from jax.experimental import pallas as pl
from jax.experimental.pallas import tpu as pltpu

D_MODEL = 2048
DEPTH = 4
POOL_WINDOWS = (2, 4, 8, 16)
N_POOL_GROUPS = len(POOL_WINDOWS)
POOL_GROUP_DIM = D_MODEL // N_POOL_GROUPS
HEAD_DIM = 128
N_HEADS = D_MODEL // HEAD_DIM
D_FF = 5504
CONV_WIDTH = 3
LN_EPS = 1e-5
DEEPNORM_ALPHA = (2.0 * DEPTH) ** 0.25

V7X_MXU_DIM = 256
V7X_BF16_SUBLANES = 16
V7X_LANES = 128
V7X_VMEM_LIMIT_BYTES = 56 * 1024 * 1024

HALO_ROWS = V7X_BF16_SUBLANES
POOL_ROWS = 512
QKV_ROWS, QKV_COLS = 1024, 512
ATTN_BLOCK = V7X_MXU_DIM
ATTN_HEADS = 4
ATTN_UNROLL = 4
PROJ_ROWS = 256
FF_PAD = 22 * V7X_MXU_DIM
UP_ROWS, UP_COLS = 1024, 512
UP_TILES = FF_PAD // UP_COLS
UP_VAL_BLOCKS = UP_COLS // V7X_LANES
DOWN_ROWS, DOWN_K = 512, FF_PAD // 4

LOG2_E = math.log2(math.e)


def _params(*semantics):
    return pltpu.CompilerParams(dimension_semantics=semantics,
                                vmem_limit_bytes=V7X_VMEM_LIMIT_BYTES)


def _residual_layer_norm(x, y, g, b):
    v = DEEPNORM_ALPHA * x + y
    mu = jnp.mean(v, axis=-1, keepdims=True)
    c = v - mu
    var = jnp.mean(c * c, axis=-1, keepdims=True)
    return c * lax.rsqrt(var + LN_EPS) * g + b


def _prev_halo_index(rows):
    per_tile = rows // HALO_ROWS
    return lambda i, *_: (jnp.maximum(i * per_tile - 1, 0), 0)


def _pool_ln_kernel(x_ref, halo_ref, w_ref, scale_ref, g_ref, b_ref, of_ref, ob_ref):
    i = pl.program_id(0)
    rows = x_ref.shape[0]
    x = x_ref[...]
    halo = jnp.where(i == 0, 0.0, halo_ref[...])
    t1 = i * rows + lax.broadcasted_iota(jnp.int32, (rows, 1), 0) + 1
    ys = []
    for grp, win in enumerate(POOL_WINDOWS):
        cols = slice(grp * POOL_GROUP_DIM, (grp + 1) * POOL_GROUP_DIM)
        xg = x[:, cols]
        s = jnp.concatenate([halo[:, cols], xg], axis=0)
        span = 1
        while span < win:
            s = s + pltpu.roll(s, span, axis=0)
            span *= 2
        cnt = jnp.minimum(t1, win).astype(jnp.float32)
        pooled = s[HALO_ROWS:, :] / cnt - xg
        ys.append(jnp.dot(pooled.astype(jnp.bfloat16), w_ref[grp].astype(jnp.bfloat16),
                          preferred_element_type=jnp.float32))
    y = jnp.concatenate(ys, axis=1) * scale_ref[...]
    out = _residual_layer_norm(x, y, g_ref[...], b_ref[...])
    of_ref[...] = out
    ob_ref[...] = out.astype(jnp.bfloat16)


def _pool_ln(x, w, layer, scale, g, b):
    seq = x.shape[0]
    row = lambda i: (i, 0)
    fixed = lambda i: (0, 0)
    return pl.pallas_call(
        _pool_ln_kernel,
        grid=(seq // POOL_ROWS,),
        in_specs=[pl.BlockSpec((POOL_ROWS, D_MODEL), row),
                  pl.BlockSpec((HALO_ROWS, D_MODEL), _prev_halo_index(POOL_ROWS)),
                  pl.BlockSpec((None,) + w.shape[1:], lambda i: (layer, 0, 0, 0)),
                  pl.BlockSpec((1, D_MODEL), fixed),
                  pl.BlockSpec((1, D_MODEL), fixed),
                  pl.BlockSpec((1, D_MODEL), fixed)],
        out_specs=[pl.BlockSpec((POOL_ROWS, D_MODEL), row),
                   pl.BlockSpec((POOL_ROWS, D_MODEL), row)],
        out_shape=[jax.ShapeDtypeStruct((seq, D_MODEL), jnp.float32),
                   jax.ShapeDtypeStruct((seq, D_MODEL), jnp.bfloat16)],
        compiler_params=_params("arbitrary"),
        name="pool_ln",
    )(x, x, w, scale, g, b)


def _qkv_kernel(x_ref, w_ref, o_ref, *, q_col_tiles, q_scale):
    j = pl.program_id(0)
    h = jnp.dot(x_ref[...], w_ref[...].astype(jnp.bfloat16), preferred_element_type=jnp.float32)
    h = h * jnp.where(j < q_col_tiles, q_scale, 1.0)
    o_ref[...] = h.astype(jnp.bfloat16)


def _qkv(xb, w, layer):
    seq = xb.shape[0]
    n_out = w.shape[2]
    kern = functools.partial(_qkv_kernel, q_col_tiles=D_MODEL // QKV_COLS,
                             q_scale=HEAD_DIM ** -0.5 * LOG2_E)
    return pl.pallas_call(
        kern,
        grid=(n_out // QKV_COLS, seq // QKV_ROWS),
        in_specs=[pl.BlockSpec((QKV_ROWS, D_MODEL), lambda j, i: (i, 0)),
                  pl.BlockSpec((None, D_MODEL, QKV_COLS), lambda j, i: (layer, 0, j))],
        out_specs=pl.BlockSpec((QKV_ROWS, QKV_COLS), lambda j, i: (i, j)),
        out_shape=jax.ShapeDtypeStruct((seq, n_out), jnp.bfloat16),
        compiler_params=_params("arbitrary", "arbitrary"),
        name="qkv",
    )(xb, w)


def _neg_abs(z):
    bits = lax.bitcast_convert_type(z, jnp.uint32) | jnp.uint32(0x80000000)
    return lax.bitcast_convert_type(bits, jnp.float32)


def _attn_kernel(q_ref, k_ref, v_ref, ntri_ref, o_ref, z_ref, a_ref, acc_ref, carry_ref):
    i = pl.program_id(1)
    blk = q_ref.shape[0]
    n_heads = q_ref.shape[1] // HEAD_DIM
    ntri = ntri_ref[...]
    head_lanes = [slice(hd * HEAD_DIM, (hd + 1) * HEAD_DIM) for hd in range(n_heads)]

    def logits(hd, kb):
        start = pl.multiple_of(kb * blk, blk)
        return lax.dot_general(q_ref[:, head_lanes[hd]], k_ref[pl.ds(start, blk), head_lanes[hd]],
                               (((1,), (1,)), ((), ())),
                               preferred_element_type=jnp.float32)

    def weigh(hd, kb):
        start = pl.multiple_of(kb * blk, blk)
        acc_ref[hd] += jnp.dot(a_ref[hd], v_ref[pl.ds(start, blk), head_lanes[hd]],
                               preferred_element_type=jnp.float32)

    def shape_all(slot):
        us, pbs = [], []
        for hd in range(n_heads):
            z = z_ref[slot, hd]
            p = jnp.maximum(z, 0.0) + jnp.log2(1.0 + jnp.exp2(_neg_abs(z)))
            us.append(z - p)
            pbs.append(p.astype(jnp.bfloat16))
        cs_all = jnp.dot(jnp.concatenate(pbs, axis=0), ntri,
                         preferred_element_type=jnp.float32)
        for hd in range(n_heads):
            cs = cs_all[hd * blk:(hd + 1) * blk]
            carry = carry_ref[hd]
            a_ref[hd] = jnp.exp2(us[hd] + cs + carry).astype(jnp.bfloat16)
            carry_ref[hd] = carry + cs[:, 0:1] - pbs[hd][:, 0:1].astype(jnp.float32)

    r = lax.broadcasted_iota(jnp.int32, (blk, blk), 0)
    c = lax.broadcasted_iota(jnp.int32, (blk, blk), 1)
    for hd in range(n_heads):
        z_ref[0, hd] = jnp.where(c < r, logits(hd, i), -jnp.inf)
        a_ref[hd] = jnp.zeros((blk, blk), jnp.bfloat16)
        acc_ref[hd] = jnp.zeros((blk, HEAD_DIM), jnp.float32)
        carry_ref[hd] = jnp.zeros((blk, 1), jnp.float32)

    def trip(t, slot):
        for hd in range(n_heads):
            weigh(hd, jnp.minimum(i - t + 1, i))
        shape_all(slot)
        for hd in range(n_heads):
            z_ref[1 - slot, hd] = logits(hd, jnp.maximum(i - t - 1, 0))

    def trips(first, count):
        for n in range(count):
            trip(first + n, n % 2)

    n_trips = i + 1
    lax.fori_loop(0, n_trips // ATTN_UNROLL,
                  lambda n, _: trips(ATTN_UNROLL * n, ATTN_UNROLL) or 0, 0)
    tail = n_trips % ATTN_UNROLL
    done = n_trips - tail
    for pairs in range(1, ATTN_UNROLL // 2):
        @pl.when(tail // 2 == pairs)
        def _():
            trips(done, 2 * pairs)

    @pl.when(tail % 2 == 1)
    def _():
        trip(i, 0)

    for hd in range(n_heads):
        weigh(hd, 0)
        o_ref[:, head_lanes[hd]] = acc_ref[hd].astype(jnp.bfloat16)


def _attention(qkv, ntri):
    seq = qkv.shape[0]
    width = ATTN_HEADS * HEAD_DIM
    groups = N_HEADS // ATTN_HEADS
    return pl.pallas_call(
        _attn_kernel,
        grid=(groups, seq // ATTN_BLOCK),
        in_specs=[pl.BlockSpec((ATTN_BLOCK, width), lambda h, i: (i, h)),
                  pl.BlockSpec((seq, width), lambda h, i: (0, groups + h)),
                  pl.BlockSpec((seq, width), lambda h, i: (0, 2 * groups + h)),
                  pl.BlockSpec((ATTN_BLOCK, ATTN_BLOCK), lambda h, i: (0, 0))],
        out_specs=pl.BlockSpec((ATTN_BLOCK, width), lambda h, i: (i, h)),
        out_shape=jax.ShapeDtypeStruct((seq, D_MODEL), jnp.bfloat16),
        scratch_shapes=[pltpu.VMEM((2, ATTN_HEADS, ATTN_BLOCK, ATTN_BLOCK), jnp.float32),
                        pltpu.VMEM((ATTN_HEADS, ATTN_BLOCK, ATTN_BLOCK), jnp.bfloat16),
                        pltpu.VMEM((ATTN_HEADS, ATTN_BLOCK, HEAD_DIM), jnp.float32),
                        pltpu.VMEM((ATTN_HEADS, ATTN_BLOCK, 1), jnp.float32)],
        compiler_params=_params("arbitrary", "arbitrary"),
        name="stickbreak_attn",
    )(qkv, qkv, qkv, ntri)


def _proj_ln_kernel(o_ref, x_ref, w_ref, g_ref, b_ref, of_ref, ob_ref):
    y = jnp.dot(o_ref[...], w_ref[...].astype(jnp.bfloat16), preferred_element_type=jnp.float32)
    out = _residual_layer_norm(x_ref[...], y, g_ref[...], b_ref[...])
    of_ref[...] = out
    ob_ref[...] = out.astype(jnp.bfloat16)


def _proj_ln(o, x, w, layer, g, b):
    seq = x.shape[0]
    row = lambda i: (i, 0)
    fixed = lambda i: (0, 0)
    return pl.pallas_call(
        _proj_ln_kernel,
        grid=(seq // PROJ_ROWS,),
        in_specs=[pl.BlockSpec((PROJ_ROWS, D_MODEL), row),
                  pl.BlockSpec((PROJ_ROWS, D_MODEL), row),
                  pl.BlockSpec((None, D_MODEL, D_MODEL), lambda i: (layer, 0, 0)),
                  pl.BlockSpec((1, D_MODEL), fixed),
                  pl.BlockSpec((1, D_MODEL), fixed)],
        out_specs=[pl.BlockSpec((PROJ_ROWS, D_MODEL), row),
                   pl.BlockSpec((PROJ_ROWS, D_MODEL), row)],
        out_shape=[jax.ShapeDtypeStruct((seq, D_MODEL), jnp.float32),
                   jax.ShapeDtypeStruct((seq, D_MODEL), jnp.bfloat16)],
        compiler_params=_params("arbitrary"),
        name="proj_ln",
    )(o, x, w, g, b)


def _ffn_up_kernel(x_ref, halo_ref, wg_ref, *rest):
    wv_refs = rest[:UP_VAL_BLOCKS]
    cwg_ref, cwv_ref, cbg_ref, cbv_ref, o_ref, xe_ref = rest[UP_VAL_BLOCKS:]
    i = pl.program_id(0)

    @pl.when(pl.program_id(1) == 0)
    def _():
        xe_ref[0:HALO_ROWS, :] = jnp.where(i == 0, jnp.zeros_like(halo_ref), halo_ref[...])
        xe_ref[HALO_ROWS:, :] = x_ref[...]

    xe = xe_ref[...]

    def causal_conv(he, cw, cb):
        h1 = pltpu.roll(he, 1, axis=0)[HALO_ROWS:]
        h2 = pltpu.roll(he, 2, axis=0)[HALO_ROWS:]
        return cb + cw[0:1] * h2 + cw[1:2] * h1 + cw[2:3] * he[HALO_ROWS:]

    wg = wg_ref[...].astype(jnp.bfloat16)
    wv = jnp.concatenate([r[...] for r in wv_refs], axis=1).astype(jnp.bfloat16)
    gate = causal_conv(jnp.dot(xe, wg, preferred_element_type=jnp.float32), cwg_ref[...], cbg_ref[...])
    val = causal_conv(jnp.dot(xe, wv, preferred_element_type=jnp.float32), cwv_ref[...], cbv_ref[...])
    o_ref[...] = (gate * jax.nn.sigmoid(gate) * val).astype(jnp.bfloat16)


def _ffn_up(xb, w_up, layer, cwg, cwv, cbg, cbv):
    seq = xb.shape[0]
    col = lambda i, j: (0, j)
    val0 = D_FF // V7X_LANES
    last = 2 * D_FF // V7X_LANES - 1

    def val_index(m, i, j):
        return layer, 0, jnp.minimum(val0 + UP_VAL_BLOCKS * j + m, last)

    val_specs = [pl.BlockSpec((None, D_MODEL, V7X_LANES), functools.partial(val_index, m))
                 for m in range(UP_VAL_BLOCKS)]
    return pl.pallas_call(
        _ffn_up_kernel,
        grid=(seq // UP_ROWS, UP_TILES),
        in_specs=[pl.BlockSpec((UP_ROWS, D_MODEL), lambda i, j: (i, 0)),
                  pl.BlockSpec((HALO_ROWS, D_MODEL), _prev_halo_index(UP_ROWS)),
                  pl.BlockSpec((None, D_MODEL, UP_COLS), lambda i, j: (layer, 0, j)),
                  *val_specs,
                  pl.BlockSpec((CONV_WIDTH, UP_COLS), col),
                  pl.BlockSpec((CONV_WIDTH, UP_COLS), col),
                  pl.BlockSpec((1, UP_COLS), col),
                  pl.BlockSpec((1, UP_COLS), col)],
        out_specs=pl.BlockSpec((UP_ROWS, UP_COLS), lambda i, j: (i, j)),
        out_shape=jax.ShapeDtypeStruct((seq, FF_PAD), jnp.bfloat16),
        scratch_shapes=[pltpu.VMEM((HALO_ROWS + UP_ROWS, D_MODEL), jnp.bfloat16)],
        compiler_params=_params("arbitrary", "arbitrary"),
        name="ffn_up",
    )(xb, xb, w_up, *([w_up] * UP_VAL_BLOCKS), cwg, cwv, cbg, cbv)


def _ffn_down_ln_kernel(a_ref, x_ref, w_ref, g_ref, b_ref, of_ref, ob_ref, acc_ref):
    k = pl.program_id(1)

    @pl.when(k == 0)
    def _():
        acc_ref[...] = jnp.zeros_like(acc_ref)

    rows = k * w_ref.shape[0] + lax.broadcasted_iota(jnp.int32, (w_ref.shape[0], 1), 0)
    w = jnp.where(rows < D_FF, w_ref[...], jnp.zeros_like(w_ref))
    acc_ref[...] += jnp.dot(a_ref[...], w, preferred_element_type=jnp.float32)

    @pl.when(k == pl.num_programs(1) - 1)
    def _():
        out = _residual_layer_norm(x_ref[...], acc_ref[...], g_ref[...], b_ref[...])
        of_ref[...] = out
        ob_ref[...] = out.astype(jnp.bfloat16)


def _ffn_down_ln(act, x, w, layer, g, b):
    seq = x.shape[0]
    row = lambda i, k: (i, 0)
    fixed = lambda i, k: (0, 0)
    return pl.pallas_call(
        _ffn_down_ln_kernel,
        grid=(seq // DOWN_ROWS, FF_PAD // DOWN_K),
        in_specs=[pl.BlockSpec((DOWN_ROWS, DOWN_K), lambda i, k: (i, k)),
                  pl.BlockSpec((DOWN_ROWS, D_MODEL), row),
                  pl.BlockSpec((None, DOWN_K, D_MODEL), lambda i, k: (layer, k, 0)),
                  pl.BlockSpec((1, D_MODEL), fixed),
                  pl.BlockSpec((1, D_MODEL), fixed)],
        out_specs=[pl.BlockSpec((DOWN_ROWS, D_MODEL), row),
                   pl.BlockSpec((DOWN_ROWS, D_MODEL), row)],
        out_shape=[jax.ShapeDtypeStruct((seq, D_MODEL), jnp.float32),
                   jax.ShapeDtypeStruct((seq, D_MODEL), jnp.bfloat16)],
        scratch_shapes=[pltpu.VMEM((DOWN_ROWS, D_MODEL), jnp.float32)],
        compiler_params=_params("arbitrary", "arbitrary"),
        name="ffn_down_ln",
    )(act, x, w, g, b)


def _pad_ff(a, axis):
    pad = [(0, 0)] * a.ndim
    pad[axis] = (0, FF_PAD - D_FF)
    return jnp.pad(a, pad)


def kernel(x, pool_w, pool_scale, attn_w_qkv, attn_w_o, ffn_w_up, ffn_conv_w, ffn_conv_b,
           ffn_w_down, ln_mix_g, ln_mix_b, ln_ffn_g, ln_ffn_b):
    batch, seq, _ = x.shape
    bf16 = jnp.bfloat16
    vec = lambda a: a.reshape(1, -1)
    ntri = -jnp.tril(jnp.ones((ATTN_BLOCK, ATTN_BLOCK), bf16), k=-1)
    w_down = ffn_w_down.astype(bf16)

    outs = []
    for bi in range(batch):
        xf = x[bi]
        xb = None
        for i in range(DEPTH):
            j = i // 2
            g, b = vec(ln_mix_g[i]), vec(ln_mix_b[i])
            if i % 2 == 0:
                xf, xb = _pool_ln(xf, pool_w, j, vec(pool_scale[j]), g, b)
            else:
                qkv = _qkv(xb, attn_w_qkv, j)
                o = _attention(qkv, ntri)
                xf, xb = _proj_ln(o, xf, attn_w_o, j, g, b)
            cw, cb = ffn_conv_w[i], ffn_conv_b[i]
            act = _ffn_up(xb, ffn_w_up, i,
                          _pad_ff(cw[:, :D_FF], 1), _pad_ff(cw[:, D_FF:], 1),
                          _pad_ff(vec(cb[:D_FF]), 1), _pad_ff(vec(cb[D_FF:]), 1))
            xf, xb = _ffn_down_ln(act, xf, w_down, i, vec(ln_ffn_g[i]), vec(ln_ffn_b[i]))
        outs.append(xf)
    return jnp.stack(outs, axis=0)
```

```python
import functools
import math

import jax
import jax.numpy as jnp
from jax import lax
from jax.experimental import pallas as pl
from jax.experimental.pallas import tpu as pltpu

D_MODEL = 2048
DEPTH = 4
POOL_WINDOWS = (2, 4, 8, 16)
N_POOL_GROUPS = len(POOL_WINDOWS)
POOL_GROUP_DIM = D_MODEL // N_POOL_GROUPS
HEAD_DIM = 128
N_HEADS = D_MODEL // HEAD_DIM
D_FF = 5504
CONV_WIDTH = 3
LN_EPS = 1e-5
DEEPNORM_ALPHA = (2.0 * DEPTH) ** 0.25

V7X_MXU_DIM = 256
V7X_BF16_SUBLANES = 16
V7X_LANES = 128
V7X_VMEM_LIMIT_BYTES = 56 * 1024 * 1024

HALO_ROWS = V7X_BF16_SUBLANES
POOL_ROWS = 512
QKV_ROWS, QKV_COLS = 1024, 512
ATTN_BLOCK = V7X_MXU_DIM
ATTN_HEADS = 4
ATTN_UNROLL = 8
PROJ_ROWS = 256
FF_PAD = 22 * V7X_MXU_DIM
UP_ROWS, UP_COLS = 1024, 512
UP_TILES = FF_PAD // UP_COLS
UP_VAL_BLOCKS = UP_COLS // V7X_LANES
DOWN_ROWS, DOWN_K = 512, FF_PAD // 4

LOG2_E = math.log2(math.e)


def _params(*semantics):
    return pltpu.CompilerParams(dimension_semantics=semantics,
                                vmem_limit_bytes=V7X_VMEM_LIMIT_BYTES)


def _residual_layer_norm(x, y, g, b):
    v = DEEPNORM_ALPHA * x + y
    mu = jnp.mean(v, axis=-1, keepdims=True)
    c = v - mu
    var = jnp.mean(c * c, axis=-1, keepdims=True)
    return c * lax.rsqrt(var + LN_EPS) * g + b


def _prev_halo_index(rows):
    per_tile = rows // HALO_ROWS
    return lambda i, *_: (jnp.maximum(i * per_tile - 1, 0), 0)


def _pool_ln_kernel(x_ref, halo_ref, w_ref, scale_ref, g_ref, b_ref, of_ref, ob_ref):
    i = pl.program_id(0)
    rows = x_ref.shape[0]
    x = x_ref[...]
    halo = jnp.where(i == 0, 0.0, halo_ref[...])
    t1 = i * rows + lax.broadcasted_iota(jnp.int32, (rows, 1), 0) + 1
    ys = []
    for grp, win in enumerate(POOL_WINDOWS):
        cols = slice(grp * POOL_GROUP_DIM, (grp + 1) * POOL_GROUP_DIM)
        xg = x[:, cols]
        s = jnp.concatenate([halo[:, cols], xg], axis=0)
        span = 1
        while span < win:
            s = s + pltpu.roll(s, span, axis=0)
            span *= 2
        cnt = jnp.minimum(t1, win).astype(jnp.float32)
        pooled = s[HALO_ROWS:, :] / cnt - xg
        ys.append(jnp.dot(pooled.astype(jnp.bfloat16), w_ref[grp].astype(jnp.bfloat16),
                          preferred_element_type=jnp.float32))
    y = jnp.concatenate(ys, axis=1) * scale_ref[...]
    out = _residual_layer_norm(x, y, g_ref[...], b_ref[...])
    of_ref[...] = out
    ob_ref[...] = out.astype(jnp.bfloat16)


def _pool_ln(x, w, layer, scale, g, b):
    seq = x.shape[0]
    row = lambda i: (i, 0)
    fixed = lambda i: (0, 0)
    return pl.pallas_call(
        _pool_ln_kernel,
        grid=(seq // POOL_ROWS,),
        in_specs=[pl.BlockSpec((POOL_ROWS, D_MODEL), row),
                  pl.BlockSpec((HALO_ROWS, D_MODEL), _prev_halo_index(POOL_ROWS)),
                  pl.BlockSpec((None,) + w.shape[1:], lambda i: (layer, 0, 0, 0)),
                  pl.BlockSpec((1, D_MODEL), fixed),
                  pl.BlockSpec((1, D_MODEL), fixed),
                  pl.BlockSpec((1, D_MODEL), fixed)],
        out_specs=[pl.BlockSpec((POOL_ROWS, D_MODEL), row),
                   pl.BlockSpec((POOL_ROWS, D_MODEL), row)],
        out_shape=[jax.ShapeDtypeStruct((seq, D_MODEL), jnp.float32),
                   jax.ShapeDtypeStruct((seq, D_MODEL), jnp.bfloat16)],
        compiler_params=_params("arbitrary"),
        name="pool_ln",
    )(x, x, w, scale, g, b)


def _qkv_kernel(x_ref, w_ref, o_ref, *, q_col_tiles, q_scale):
    j = pl.program_id(0)
    h = jnp.dot(x_ref[...], w_ref[...].astype(jnp.bfloat16), preferred_element_type=jnp.float32)
    h = h * jnp.where(j < q_col_tiles, q_scale, 1.0)
    o_ref[...] = h.astype(jnp.bfloat16)


def _qkv(xb, w, layer):
    seq = xb.shape[0]
    n_out = w.shape[2]
    kern = functools.partial(_qkv_kernel, q_col_tiles=D_MODEL // QKV_COLS,
                             q_scale=HEAD_DIM ** -0.5 * LOG2_E)
    return pl.pallas_call(
        kern,
        grid=(n_out // QKV_COLS, seq // QKV_ROWS),
        in_specs=[pl.BlockSpec((QKV_ROWS, D_MODEL), lambda j, i: (i, 0)),
                  pl.BlockSpec((None, D_MODEL, QKV_COLS), lambda j, i: (layer, 0, j))],
        out_specs=pl.BlockSpec((QKV_ROWS, QKV_COLS), lambda j, i: (i, j)),
        out_shape=jax.ShapeDtypeStruct((seq, n_out), jnp.bfloat16),
        compiler_params=_params("arbitrary", "arbitrary"),
        name="qkv",
    )(xb, w)


def _neg_abs(z):
    bits = lax.bitcast_convert_type(z, jnp.uint32) | jnp.uint32(0x80000000)
    return lax.bitcast_convert_type(bits, jnp.float32)


def _attn_kernel(q_ref, k_ref, v_ref, ntri_ref, o_ref, z_ref, a_ref, acc_ref, carry_ref):
    i = pl.program_id(1)
    blk = q_ref.shape[0]
    n_heads = q_ref.shape[1] // HEAD_DIM
    ntri = ntri_ref[...]
    head_lanes = [slice(hd * HEAD_DIM, (hd + 1) * HEAD_DIM) for hd in range(n_heads)]

    def logits(hd, kb):
        start = pl.multiple_of(kb * blk, blk)
        return lax.dot_general(q_ref[:, head_lanes[hd]], k_ref[pl.ds(start, blk), head_lanes[hd]],
                               (((1,), (1,)), ((), ())),
                               preferred_element_type=jnp.float32)

    def weigh(hd, kb):
        start = pl.multiple_of(kb * blk, blk)
        acc_ref[hd] += jnp.dot(a_ref[hd], v_ref[pl.ds(start, blk), head_lanes[hd]],
                               preferred_element_type=jnp.float32)

    def shape_all(slot):
        us, pbs = [], []
        for hd in range(n_heads):
            z = z_ref[slot, hd]
            p = jnp.maximum(z, 0.0) + jnp.log2(1.0 + jnp.exp2(_neg_abs(z)))
            us.append(z - p)
            pbs.append(p)
        cs_all = jnp.dot(jnp.concatenate(pbs, axis=0), ntri,
                         preferred_element_type=jnp.float32)
        for hd in range(n_heads):
            cs = cs_all[hd * blk:(hd + 1) * blk]
            carry = carry_ref[hd]
            a_ref[hd] = jnp.exp2(us[hd] + cs + carry).astype(jnp.bfloat16)
            carry_ref[hd] = carry + cs[:, 0:1] - pbs[hd][:, 0:1]

    r = lax.broadcasted_iota(jnp.int32, (blk, blk), 0)
    c = lax.broadcasted_iota(jnp.int32, (blk, blk), 1)
    for hd in range(n_heads):
        z_ref[0, hd] = jnp.where(c < r, logits(hd, i), -jnp.inf)
        a_ref[hd] = jnp.zeros((blk, blk), jnp.bfloat16)
        acc_ref[hd] = jnp.zeros((blk, HEAD_DIM), jnp.float32)
        carry_ref[hd] = jnp.zeros((blk, 1), jnp.float32)

    def trip(t, slot):
        for hd in range(n_heads):
            weigh(hd, jnp.minimum(i - t + 1, i))
        shape_all(slot)
        for hd in range(n_heads):
            z_ref[1 - slot, hd] = logits(hd, jnp.maximum(i - t - 1, 0))

    def trips(first, count):
        for n in range(count):
            trip(first + n, n % 2)

    n_trips = i + 1
    lax.fori_loop(0, n_trips // ATTN_UNROLL,
                  lambda n, _: trips(ATTN_UNROLL * n, ATTN_UNROLL) or 0, 0)
    tail = n_trips % ATTN_UNROLL
    done = n_trips - tail
    for pairs in range(1, ATTN_UNROLL // 2):
        @pl.when(tail // 2 == pairs)
        def _():
            trips(done, 2 * pairs)

    @pl.when(tail % 2 == 1)
    def _():
        trip(i, 0)

    for hd in range(n_heads):
        weigh(hd, 0)
        o_ref[:, head_lanes[hd]] = acc_ref[hd].astype(jnp.bfloat16)


def _attention(qkv, ntri):
    seq = qkv.shape[0]
    width = ATTN_HEADS * HEAD_DIM
    groups = N_HEADS // ATTN_HEADS
    return pl.pallas_call(
        _attn_kernel,
        grid=(groups, seq // ATTN_BLOCK),
        in_specs=[pl.BlockSpec((ATTN_BLOCK, width), lambda h, i: (i, h)),
                  pl.BlockSpec((seq, width), lambda h, i: (0, groups + h)),
                  pl.BlockSpec((seq, width), lambda h, i: (0, 2 * groups + h)),
                  pl.BlockSpec((ATTN_BLOCK, ATTN_BLOCK), lambda h, i: (0, 0))],
        out_specs=pl.BlockSpec((ATTN_BLOCK, width), lambda h, i: (i, h)),
        out_shape=jax.ShapeDtypeStruct((seq, D_MODEL), jnp.bfloat16),
        scratch_shapes=[pltpu.VMEM((2, ATTN_HEADS, ATTN_BLOCK, ATTN_BLOCK), jnp.float32),
                        pltpu.VMEM((ATTN_HEADS, ATTN_BLOCK, ATTN_BLOCK), jnp.bfloat16),
                        pltpu.VMEM((ATTN_HEADS, ATTN_BLOCK, HEAD_DIM), jnp.float32),
                        pltpu.VMEM((ATTN_HEADS, ATTN_BLOCK, 1), jnp.float32)],
        compiler_params=_params("arbitrary", "arbitrary"),
        name="stickbreak_attn",
    )(qkv, qkv, qkv, ntri)


def _proj_ln_kernel(o_ref, x_ref, w_ref, g_ref, b_ref, of_ref, ob_ref):
    y = jnp.dot(o_ref[...], w_ref[...].astype(jnp.bfloat16), preferred_element_type=jnp.float32)
    out = _residual_layer_norm(x_ref[...], y, g_ref[...], b_ref[...])
    of_ref[...] = out
    ob_ref[...] = out.astype(jnp.bfloat16)


def _proj_ln(o, x, w, layer, g, b):
    seq = x.shape[0]
    row = lambda i: (i, 0)
    fixed = lambda i: (0, 0)
    return pl.pallas_call(
        _proj_ln_kernel,
        grid=(seq // PROJ_ROWS,),
        in_specs=[pl.BlockSpec((PROJ_ROWS, D_MODEL), row),
                  pl.BlockSpec((PROJ_ROWS, D_MODEL), row),
                  pl.BlockSpec((None, D_MODEL, D_MODEL), lambda i: (layer, 0, 0)),
                  pl.BlockSpec((1, D_MODEL), fixed),
                  pl.BlockSpec((1, D_MODEL), fixed)],
        out_specs=[pl.BlockSpec((PROJ_ROWS, D_MODEL), row),
                   pl.BlockSpec((PROJ_ROWS, D_MODEL), row)],
        out_shape=[jax.ShapeDtypeStruct((seq, D_MODEL), jnp.float32),
                   jax.ShapeDtypeStruct((seq, D_MODEL), jnp.bfloat16)],
        compiler_params=_params("arbitrary"),
        name="proj_ln",
    )(o, x, w, g, b)


def _ffn_up_kernel(x_ref, halo_ref, wg_ref, *rest):
    wv_refs = rest[:UP_VAL_BLOCKS]
    cwg_ref, cwv_ref, cbg_ref, cbv_ref, o_ref, xe_ref = rest[UP_VAL_BLOCKS:]
    i = pl.program_id(0)

    @pl.when(pl.program_id(1) == 0)
    def _():
        xe_ref[0:HALO_ROWS, :] = jnp.where(i == 0, jnp.zeros_like(halo_ref), halo_ref[...])
        xe_ref[HALO_ROWS:, :] = x_ref[...]

    xe = xe_ref[...]

    def causal_conv(he, cw, cb):
        h1 = pltpu.roll(he, 1, axis=0)[HALO_ROWS:]
        h2 = pltpu.roll(he, 2, axis=0)[HALO_ROWS:]
        return cb + cw[0:1] * h2 + cw[1:2] * h1 + cw[2:3] * he[HALO_ROWS:]

    wg = wg_ref[...].astype(jnp.bfloat16)
    wv = jnp.concatenate([r[...] for r in wv_refs], axis=1).astype(jnp.bfloat16)
    gate = causal_conv(jnp.dot(xe, wg, preferred_element_type=jnp.float32), cwg_ref[...], cbg_ref[...])
    val = causal_conv(jnp.dot(xe, wv, preferred_element_type=jnp.float32), cwv_ref[...], cbv_ref[...])
    o_ref[...] = (gate * jax.nn.sigmoid(gate) * val).astype(jnp.bfloat16)


def _ffn_up(xb, w_up, layer, cwg, cwv, cbg, cbv):
    seq = xb.shape[0]
    col = lambda i, j: (0, j)
    val0 = D_FF // V7X_LANES
    last = 2 * D_FF // V7X_LANES - 1

    def val_index(m, i, j):
        return layer, 0, jnp.minimum(val0 + UP_VAL_BLOCKS * j + m, last)

    val_specs = [pl.BlockSpec((None, D_MODEL, V7X_LANES), functools.partial(val_index, m))
                 for m in range(UP_VAL_BLOCKS)]
    return pl.pallas_call(
        _ffn_up_kernel,
        grid=(seq // UP_ROWS, UP_TILES),
        in_specs=[pl.BlockSpec((UP_ROWS, D_MODEL), lambda i, j: (i, 0)),
                  pl.BlockSpec((HALO_ROWS, D_MODEL), _prev_halo_index(UP_ROWS)),
                  pl.BlockSpec((None, D_MODEL, UP_COLS), lambda i, j: (layer, 0, j)),
                  *val_specs,
                  pl.BlockSpec((CONV_WIDTH, UP_COLS), col),
                  pl.BlockSpec((CONV_WIDTH, UP_COLS), col),
                  pl.BlockSpec((1, UP_COLS), col),
                  pl.BlockSpec((1, UP_COLS), col)],
        out_specs=pl.BlockSpec((UP_ROWS, UP_COLS), lambda i, j: (i, j)),
        out_shape=jax.ShapeDtypeStruct((seq, FF_PAD), jnp.bfloat16),
        scratch_shapes=[pltpu.VMEM((HALO_ROWS + UP_ROWS, D_MODEL), jnp.bfloat16)],
        compiler_params=_params("arbitrary", "arbitrary"),
        name="ffn_up",
    )(xb, xb, w_up, *([w_up] * UP_VAL_BLOCKS), cwg, cwv, cbg, cbv)


def _ffn_down_ln_kernel(a_ref, x_ref, w_ref, g_ref, b_ref, of_ref, ob_ref, acc_ref):
    k = pl.program_id(1)

    @pl.when(k == 0)
    def _():
        acc_ref[...] = jnp.zeros_like(acc_ref)

    rows = k * w_ref.shape[0] + lax.broadcasted_iota(jnp.int32, (w_ref.shape[0], 1), 0)
    w = jnp.where(rows < D_FF, w_ref[...], jnp.zeros_like(w_ref))
    acc_ref[...] += jnp.dot(a_ref[...], w, preferred_element_type=jnp.float32)

    @pl.when(k == pl.num_programs(1) - 1)
    def _():
        out = _residual_layer_norm(x_ref[...], acc_ref[...], g_ref[...], b_ref[...])
        of_ref[...] = out
        ob_ref[...] = out.astype(jnp.bfloat16)


def _ffn_down_ln(act, x, w, layer, g, b):
    seq = x.shape[0]
    row = lambda i, k: (i, 0)
    fixed = lambda i, k: (0, 0)
    return pl.pallas_call(
        _ffn_down_ln_kernel,
        grid=(seq // DOWN_ROWS, FF_PAD // DOWN_K),
        in_specs=[pl.BlockSpec((DOWN_ROWS, DOWN_K), lambda i, k: (i, k)),
                  pl.BlockSpec((DOWN_ROWS, D_MODEL), row),
                  pl.BlockSpec((None, DOWN_K, D_MODEL), lambda i, k: (layer, k, 0)),
                  pl.BlockSpec((1, D_MODEL), fixed),
                  pl.BlockSpec((1, D_MODEL), fixed)],
        out_specs=[pl.BlockSpec((DOWN_ROWS, D_MODEL), row),
                   pl.BlockSpec((DOWN_ROWS, D_MODEL), row)],
        out_shape=[jax.ShapeDtypeStruct((seq, D_MODEL), jnp.float32),
                   jax.ShapeDtypeStruct((seq, D_MODEL), jnp.bfloat16)],
        scratch_shapes=[pltpu.VMEM((DOWN_ROWS, D_MODEL), jnp.float32)],
        compiler_params=_params("arbitrary", "arbitrary"),
        name="ffn_down_ln",
    )(act, x, w, g, b)


def _pad_ff(a, axis):
    pad = [(0, 0)] * a.ndim
    pad[axis] = (0, FF_PAD - D_FF)
    return jnp.pad(a, pad)


def kernel(x, pool_w, pool_scale, attn_w_qkv, attn_w_o, ffn_w_up, ffn_conv_w, ffn_conv_b,
           ffn_w_down, ln_mix_g, ln_mix_b, ln_ffn_g, ln_ffn_b):
    batch, seq, _ = x.shape
    bf16 = jnp.bfloat16
    vec = lambda a: a.reshape(1, -1)
    ntri = -jnp.tril(jnp.ones((ATTN_BLOCK, ATTN_BLOCK), jnp.float32), k=-1)
    w_down = ffn_w_down.astype(bf16)

    outs = []
    for bi in range(batch):
        xf = x[bi]
        xb = None
        for i in range(DEPTH):
            j = i // 2
            g, b = vec(ln_mix_g[i]), vec(ln_mix_b[i])
            if i % 2 == 0:
                xf, xb = _pool_ln(xf, pool_w, j, vec(pool_scale[j]), g, b)
            else:
                qkv = _qkv(xb, attn_w_qkv, j)
                o = _attention(qkv, ntri)
                xf, xb = _proj_ln(o, xf, attn_w_o, j, g, b)
            cw, cb = ffn_conv_w[i], ffn_conv_b[i]
            act = _ffn_up(xb, ffn_w_up, i,
                          _pad_ff(cw[:, :D_FF], 1), _pad_ff(cw[:, D_FF:], 1),
                          _pad_ff(vec(cb[:D_FF]), 1), _pad_ff(vec(cb[D_FF:]), 1))
            xf, xb = _ffn_down_ln(act, xf, w_down, i, vec(ln_ffn_g[i]), vec(ln_ffn_b[i]))
        outs.append(xf)
    return jnp.stack(outs, axis=0)
```

```python
import functools
import math

import jax
import jax.numpy as jnp
from jax import lax
from jax.experimental import pallas as pl
from jax.experimental.pallas import tpu as pltpu

D_MODEL = 2048
DEPTH = 4
POOL_WINDOWS = (2, 4, 8, 16)
N_POOL_GROUPS = len(POOL_WINDOWS)
POOL_GROUP_DIM = D_MODEL // N_POOL_GROUPS
HEAD_DIM = 128
N_HEADS = D_MODEL // HEAD_DIM
D_FF = 5504
CONV_WIDTH = 3
LN_EPS = 1e-5
DEEPNORM_ALPHA = (2.0 * DEPTH) ** 0.25

V7X_MXU_DIM = 256
V7X_BF16_SUBLANES = 16
V7X_LANES = 128
V7X_VMEM_LIMIT_BYTES = 56 * 1024 * 1024

HALO_ROWS = V7X_BF16_SUBLANES
POOL_ROWS = 512
QKV_ROWS, QKV_COLS = 1024, 1024
ATTN_BLOCK = V7X_MXU_DIM
ATTN_HEADS = 4
ATTN_UNROLL = 8
PROJ_ROWS = 256
FF_PAD = 22 * V7X_MXU_DIM
UP_ROWS, UP_COLS = 1024, 512
UP_TILES = FF_PAD // UP_COLS
UP_VAL_BLOCKS = UP_COLS // V7X_LANES
DOWN_STEPS = 4
DOWN_ROWS, DOWN_K = 512, FF_PAD // DOWN_STEPS

LOG2_E = math.log2(math.e)


def _params(*semantics):
    return pltpu.CompilerParams(dimension_semantics=semantics,
                                vmem_limit_bytes=V7X_VMEM_LIMIT_BYTES)


def _residual_layer_norm(x, y, g, b):
    v = DEEPNORM_ALPHA * x + y
    mu = jnp.mean(v, axis=-1, keepdims=True)
    c = v - mu
    var = jnp.mean(c * c, axis=-1, keepdims=True)
    return c * lax.rsqrt(var + LN_EPS) * g + b


def _prev_halo_index(rows):
    per_tile = rows // HALO_ROWS
    return lambda i, *_: (jnp.maximum(i * per_tile - 1, 0), 0)


def _pool_ln_kernel(x_ref, halo_ref, w_ref, scale_ref, g_ref, b_ref, of_ref, ob_ref):
    i = pl.program_id(0)
    rows = x_ref.shape[0]
    x = x_ref[...]
    halo = jnp.where(i == 0, 0.0, halo_ref[...])
    t1 = i * rows + lax.broadcasted_iota(jnp.int32, (rows, 1), 0) + 1
    ys = []
    for grp, win in enumerate(POOL_WINDOWS):
        cols = slice(grp * POOL_GROUP_DIM, (grp + 1) * POOL_GROUP_DIM)
        xg = x[:, cols]
        s = jnp.concatenate([halo[:, cols], xg], axis=0)
        span = 1
        while span < win:
            s = s + pltpu.roll(s, span, axis=0)
            span *= 2
        cnt = jnp.minimum(t1, win).astype(jnp.float32)
        pooled = s[HALO_ROWS:, :] / cnt - xg
        ys.append(jnp.dot(pooled.astype(jnp.bfloat16), w_ref[grp].astype(jnp.bfloat16),
                          preferred_element_type=jnp.float32))
    y = jnp.concatenate(ys, axis=1) * scale_ref[...]
    out = _residual_layer_norm(x, y, g_ref[...], b_ref[...])
    of_ref[...] = out
    ob_ref[...] = out.astype(jnp.bfloat16)


def _pool_ln(x, w, layer, scale, g, b):
    seq = x.shape[0]
    row = lambda i: (i, 0)
    fixed = lambda i: (0, 0)
    return pl.pallas_call(
        _pool_ln_kernel,
        grid=(seq // POOL_ROWS,),
        in_specs=[pl.BlockSpec((POOL_ROWS, D_MODEL), row),
                  pl.BlockSpec((HALO_ROWS, D_MODEL), _prev_halo_index(POOL_ROWS)),
                  pl.BlockSpec((None,) + w.shape[1:], lambda i: (layer, 0, 0, 0)),
                  pl.BlockSpec((1, D_MODEL), fixed),
                  pl.BlockSpec((1, D_MODEL), fixed),
                  pl.BlockSpec((1, D_MODEL), fixed)],
        out_specs=[pl.BlockSpec((POOL_ROWS, D_MODEL), row),
                   pl.BlockSpec((POOL_ROWS, D_MODEL), row)],
        out_shape=[jax.ShapeDtypeStruct((seq, D_MODEL), jnp.float32),
                   jax.ShapeDtypeStruct((seq, D_MODEL), jnp.bfloat16)],
        compiler_params=_params("arbitrary"),
        name="pool_ln",
    )(x, x, w, scale, g, b)


def _qkv_kernel(x_ref, w_ref, o_ref, *, q_col_tiles, q_scale):
    j = pl.program_id(0)
    h = jnp.dot(x_ref[...], w_ref[...].astype(jnp.bfloat16), preferred_element_type=jnp.float32)
    h = h * jnp.where(j < q_col_tiles, q_scale, 1.0)
    o_ref[...] = h.astype(jnp.bfloat16)


def _qkv(xb, w, layer):
    seq = xb.shape[0]
    n_out = w.shape[2]
    kern = functools.partial(_qkv_kernel, q_col_tiles=D_MODEL // QKV_COLS,
                             q_scale=HEAD_DIM ** -0.5 * LOG2_E)
    return pl.pallas_call(
        kern,
        grid=(n_out // QKV_COLS, seq // QKV_ROWS),
        in_specs=[pl.BlockSpec((QKV_ROWS, D_MODEL), lambda j, i: (i, 0)),
                  pl.BlockSpec((None, D_MODEL, QKV_COLS), lambda j, i: (layer, 0, j))],
        out_specs=pl.BlockSpec((QKV_ROWS, QKV_COLS), lambda j, i: (i, j)),
        out_shape=jax.ShapeDtypeStruct((seq, n_out), jnp.bfloat16),
        compiler_params=_params("arbitrary", "arbitrary"),
        name="qkv",
    )(xb, w)


def _neg_abs(z):
    bits = lax.bitcast_convert_type(z, jnp.uint32) | jnp.uint32(0x80000000)
    return lax.bitcast_convert_type(bits, jnp.float32)


def _attn_kernel(q_ref, k_ref, v_ref, ntri_ref, o_ref, z_ref, a_ref, acc_ref, carry_ref):
    i = pl.program_id(1)
    blk = q_ref.shape[0]
    n_heads = q_ref.shape[1] // HEAD_DIM
    ntri = ntri_ref[...]
    head_lanes = [slice(hd * HEAD_DIM, (hd + 1) * HEAD_DIM) for hd in range(n_heads)]

    def logits(hd, kb):
        start = pl.multiple_of(kb * blk, blk)
        return lax.dot_general(q_ref[:, head_lanes[hd]], k_ref[pl.ds(start, blk), head_lanes[hd]],
                               (((1,), (1,)), ((), ())),
                               preferred_element_type=jnp.float32)

    def weigh(hd, kb):
        start = pl.multiple_of(kb * blk, blk)
        acc_ref[hd] += jnp.dot(a_ref[hd], v_ref[pl.ds(start, blk), head_lanes[hd]],
                               preferred_element_type=jnp.float32)

    def shape_all(slot):
        us, pbs = [], []
        for hd in range(n_heads):
            z = z_ref[slot, hd]
            p = jnp.maximum(z, 0.0) + jnp.log2(1.0 + jnp.exp2(_neg_abs(z)))
            us.append(z - p)
            pbs.append(p)
        cs_all = jnp.dot(jnp.concatenate(pbs, axis=0), ntri,
                         preferred_element_type=jnp.float32)
        for hd in range(n_heads):
            cs = cs_all[hd * blk:(hd + 1) * blk]
            carry = carry_ref[hd]
            a_ref[hd] = jnp.exp2(us[hd] + cs + carry).astype(jnp.bfloat16)
            carry_ref[hd] = carry + cs[:, 0:1] - pbs[hd][:, 0:1]

    r = lax.broadcasted_iota(jnp.int32, (blk, blk), 0)
    c = lax.broadcasted_iota(jnp.int32, (blk, blk), 1)
    for hd in range(n_heads):
        z_ref[0, hd] = jnp.where(c < r, logits(hd, i), -jnp.inf)
        a_ref[hd] = jnp.zeros((blk, blk), jnp.bfloat16)
        acc_ref[hd] = jnp.zeros((blk, HEAD_DIM), jnp.float32)
        carry_ref[hd] = jnp.zeros((blk, 1), jnp.float32)

    def trip(t, slot):
        for hd in range(n_heads):
            weigh(hd, jnp.minimum(i - t + 1, i))
        shape_all(slot)
        for hd in range(n_heads):
            z_ref[1 - slot, hd] = logits(hd, jnp.maximum(i - t - 1, 0))

    def trips(first, count):
        for n in range(count):
            trip(first + n, n % 2)

    n_trips = i + 1
    lax.fori_loop(0, n_trips // ATTN_UNROLL,
                  lambda n, _: trips(ATTN_UNROLL * n, ATTN_UNROLL) or 0, 0)
    tail = n_trips % ATTN_UNROLL
    done = n_trips - tail
    for pairs in range(1, ATTN_UNROLL // 2):
        @pl.when(tail // 2 == pairs)
        def _():
            trips(done, 2 * pairs)

    @pl.when(tail % 2 == 1)
    def _():
        trip(i, 0)

    for hd in range(n_heads):
        weigh(hd, 0)
        o_ref[:, head_lanes[hd]] = acc_ref[hd].astype(jnp.bfloat16)


def _attention(qkv, ntri):
    seq = qkv.shape[0]
    width = ATTN_HEADS * HEAD_DIM
    groups = N_HEADS // ATTN_HEADS
    return pl.pallas_call(
        _attn_kernel,
        grid=(groups, seq // ATTN_BLOCK),
        in_specs=[pl.BlockSpec((ATTN_BLOCK, width), lambda h, i: (i, h)),
                  pl.BlockSpec((seq, width), lambda h, i: (0, groups + h)),
                  pl.BlockSpec((seq, width), lambda h, i: (0, 2 * groups + h)),
                  pl.BlockSpec((ATTN_BLOCK, ATTN_BLOCK), lambda h, i: (0, 0))],
        out_specs=pl.BlockSpec((ATTN_BLOCK, width), lambda h, i: (i, h)),
        out_shape=jax.ShapeDtypeStruct((seq, D_MODEL), jnp.bfloat16),
        scratch_shapes=[pltpu.VMEM((2, ATTN_HEADS, ATTN_BLOCK, ATTN_BLOCK), jnp.float32),
                        pltpu.VMEM((ATTN_HEADS, ATTN_BLOCK, ATTN_BLOCK), jnp.bfloat16),
                        pltpu.VMEM((ATTN_HEADS, ATTN_BLOCK, HEAD_DIM), jnp.float32),
                        pltpu.VMEM((ATTN_HEADS, ATTN_BLOCK, 1), jnp.float32)],
        compiler_params=_params("arbitrary", "arbitrary"),
        name="stickbreak_attn",
    )(qkv, qkv, qkv, ntri)


def _proj_ln_kernel(o_ref, x_ref, w_ref, g_ref, b_ref, of_ref, ob_ref):
    y = jnp.dot(o_ref[...], w_ref[...].astype(jnp.bfloat16), preferred_element_type=jnp.float32)
    out = _residual_layer_norm(x_ref[...], y, g_ref[...], b_ref[...])
    of_ref[...] = out
    ob_ref[...] = out.astype(jnp.bfloat16)


def _proj_ln(o, x, w, layer, g, b):
    seq = x.shape[0]
    row = lambda i: (i, 0)
    fixed = lambda i: (0, 0)
    return pl.pallas_call(
        _proj_ln_kernel,
        grid=(seq // PROJ_ROWS,),
        in_specs=[pl.BlockSpec((PROJ_ROWS, D_MODEL), row),
                  pl.BlockSpec((PROJ_ROWS, D_MODEL), row),
                  pl.BlockSpec((None, D_MODEL, D_MODEL), lambda i: (layer, 0, 0)),
                  pl.BlockSpec((1, D_MODEL), fixed),
                  pl.BlockSpec((1, D_MODEL), fixed)],
        out_specs=[pl.BlockSpec((PROJ_ROWS, D_MODEL), row),
                   pl.BlockSpec((PROJ_ROWS, D_MODEL), row)],
        out_shape=[jax.ShapeDtypeStruct((seq, D_MODEL), jnp.float32),
                   jax.ShapeDtypeStruct((seq, D_MODEL), jnp.bfloat16)],
        compiler_params=_params("arbitrary"),
        name="proj_ln",
    )(o, x, w, g, b)


def _ffn_up_kernel(x_ref, halo_ref, wg_ref, *rest):
    wv_refs = rest[:UP_VAL_BLOCKS]
    cwg_ref, cwv_ref, cbg_ref, cbv_ref, o_ref, xe_ref = rest[UP_VAL_BLOCKS:]
    i = pl.program_id(0)

    @pl.when(pl.program_id(1) == 0)
    def _():
        xe_ref[0:HALO_ROWS, :] = jnp.where(i == 0, jnp.zeros_like(halo_ref), halo_ref[...])
        xe_ref[HALO_ROWS:, :] = x_ref[...]

    xe = xe_ref[...]

    def causal_conv(he, cw, cb):
        h1 = pltpu.roll(he, 1, axis=0)[HALO_ROWS:]
        h2 = pltpu.roll(he, 2, axis=0)[HALO_ROWS:]
        return cb + cw[0:1] * h2 + cw[1:2] * h1 + cw[2:3] * he[HALO_ROWS:]

    wg = wg_ref[...].astype(jnp.bfloat16)
    wv = jnp.concatenate([r[...] for r in wv_refs], axis=1).astype(jnp.bfloat16)
    gate = causal_conv(jnp.dot(xe, wg, preferred_element_type=jnp.float32), cwg_ref[...], cbg_ref[...])
    val = causal_conv(jnp.dot(xe, wv, preferred_element_type=jnp.float32), cwv_ref[...], cbv_ref[...])
    o_ref[...] = (gate * jax.nn.sigmoid(gate) * val).astype(jnp.bfloat16)


def _ffn_up(xb, w_up, layer, cwg, cwv, cbg, cbv):
    seq = xb.shape[0]
    col = lambda i, j: (0, j)
    val0 = D_FF // V7X_LANES
    last = 2 * D_FF // V7X_LANES - 1

    def val_index(m, i, j):
        return layer, 0, jnp.minimum(val0 + UP_VAL_BLOCKS * j + m, last)

    val_specs = [pl.BlockSpec((None, D_MODEL, V7X_LANES), functools.partial(val_index, m))
                 for m in range(UP_VAL_BLOCKS)]
    return pl.pallas_call(
        _ffn_up_kernel,
        grid=(seq // UP_ROWS, UP_TILES),
        in_specs=[pl.BlockSpec((UP_ROWS, D_MODEL), lambda i, j: (i, 0)),
                  pl.BlockSpec((HALO_ROWS, D_MODEL), _prev_halo_index(UP_ROWS)),
                  pl.BlockSpec((None, D_MODEL, UP_COLS), lambda i, j: (layer, 0, j)),
                  *val_specs,
                  pl.BlockSpec((CONV_WIDTH, UP_COLS), col),
                  pl.BlockSpec((CONV_WIDTH, UP_COLS), col),
                  pl.BlockSpec((1, UP_COLS), col),
                  pl.BlockSpec((1, UP_COLS), col)],
        out_specs=pl.BlockSpec((UP_ROWS, UP_COLS), lambda i, j: (i, j)),
        out_shape=jax.ShapeDtypeStruct((seq, FF_PAD), jnp.bfloat16),
        scratch_shapes=[pltpu.VMEM((HALO_ROWS + UP_ROWS, D_MODEL), jnp.bfloat16)],
        compiler_params=_params("arbitrary", "arbitrary"),
        name="ffn_up",
    )(xb, xb, w_up, *([w_up] * UP_VAL_BLOCKS), cwg, cwv, cbg, cbv)


def _ffn_down_ln_kernel(a_ref, x_ref, w_ref, g_ref, b_ref, of_ref, ob_ref, acc_ref):
    k = pl.program_id(1)

    @pl.when(k == 0)
    def _():
        acc_ref[...] = jnp.zeros_like(acc_ref)

    last = DOWN_STEPS - 1

    @pl.when(k < last)
    def _():
        acc_ref[...] += jnp.dot(a_ref[...], w_ref[...], preferred_element_type=jnp.float32)

    @pl.when(k == last)
    def _():
        kk = D_FF - last * DOWN_K
        w = w_ref[0:kk, :]
        half = a_ref.shape[0] // 2
        outs = []
        for r in range(2):
            rows = slice(r * half, (r + 1) * half)
            y = acc_ref[rows, :] + jnp.dot(a_ref[rows, 0:kk], w, preferred_element_type=jnp.float32)
            outs.append(_residual_layer_norm(x_ref[rows, :], y, g_ref[...], b_ref[...]))
        out = jnp.concatenate(outs, axis=0)
        of_ref[...] = out
        ob_ref[...] = out.astype(jnp.bfloat16)


def _ffn_down_ln(act, x, w, layer, g, b):
    seq = x.shape[0]
    row = lambda i, k: (i, 0)
    fixed = lambda i, k: (0, 0)
    return pl.pallas_call(
        _ffn_down_ln_kernel,
        grid=(seq // DOWN_ROWS, DOWN_STEPS),
        in_specs=[pl.BlockSpec((DOWN_ROWS, DOWN_K), lambda i, k: (i, k)),
                  pl.BlockSpec((DOWN_ROWS, D_MODEL), row),
                  pl.BlockSpec((None, DOWN_K, D_MODEL), lambda i, k: (layer, k, 0)),
                  pl.BlockSpec((1, D_MODEL), fixed),
                  pl.BlockSpec((1, D_MODEL), fixed)],
        out_specs=[pl.BlockSpec((DOWN_ROWS, D_MODEL), row),
                   pl.BlockSpec((DOWN_ROWS, D_MODEL), row)],
        out_shape=[jax.ShapeDtypeStruct((seq, D_MODEL), jnp.float32),
                   jax.ShapeDtypeStruct((seq, D_MODEL), jnp.bfloat16)],
        scratch_shapes=[pltpu.VMEM((DOWN_ROWS, D_MODEL), jnp.float32)],
        compiler_params=_params("arbitrary", "arbitrary"),
        name="ffn_down_ln",
    )(act, x, w, g, b)


def _pad_ff(a, axis):
    pad = [(0, 0)] * a.ndim
    pad[axis] = (0, FF_PAD - D_FF)
    return jnp.pad(a, pad)


def kernel(x, pool_w, pool_scale, attn_w_qkv, attn_w_o, ffn_w_up, ffn_conv_w, ffn_conv_b,
           ffn_w_down, ln_mix_g, ln_mix_b, ln_ffn_g, ln_ffn_b):
    batch, seq, _ = x.shape
    bf16 = jnp.bfloat16
    vec = lambda a: a.reshape(1, -1)
    ntri = -jnp.tril(jnp.ones((ATTN_BLOCK, ATTN_BLOCK), jnp.float32), k=-1)
    w_down = ffn_w_down.astype(bf16)

    outs = []
    for bi in range(batch):
        xf = x[bi]
        xb = None
        for i in range(DEPTH):
            j = i // 2
            g, b = vec(ln_mix_g[i]), vec(ln_mix_b[i])
            if i % 2 == 0:
                xf, xb = _pool_ln(xf, pool_w, j, vec(pool_scale[j]), g, b)
            else:
                qkv = _qkv(xb, attn_w_qkv, j)
                o = _attention(qkv, ntri)
                xf, xb = _proj_ln(o, xf, attn_w_o, j, g, b)
            cw, cb = ffn_conv_w[i], ffn_conv_b[i]
            act = _ffn_up(xb, ffn_w_up, i,
                          _pad_ff(cw[:, :D_FF], 1), _pad_ff(cw[:, D_FF:], 1),
                          _pad_ff(vec(cb[:D_FF]), 1), _pad_ff(vec(cb[D_FF:]), 1))
            xf, xb = _ffn_down_ln(act, xf, w_down, i, vec(ln_ffn_g[i]), vec(ln_ffn_b[i]))
        outs.append(xf)
    return jnp.stack(outs, axis=0)
```

```python
import functools
import math

import jax
import jax.numpy as jnp
from jax import lax
from jax.experimental import pallas as pl
from jax.experimental.pallas import tpu as pltpu

D_MODEL = 2048
DEPTH = 4
POOL_WINDOWS = (2, 4, 8, 16)
N_POOL_GROUPS = len(POOL_WINDOWS)
POOL_GROUP_DIM = D_MODEL // N_POOL_GROUPS
HEAD_DIM = 128
N_HEADS = D_MODEL // HEAD_DIM
D_FF = 5504
CONV_WIDTH = 3
LN_EPS = 1e-5
DEEPNORM_ALPHA = (2.0 * DEPTH) ** 0.25

V7X_MXU_DIM = 256
V7X_BF16_SUBLANES = 16
V7X_LANES = 128
V7X_VMEM_LIMIT_BYTES = 56 * 1024 * 1024

HALO_ROWS = V7X_BF16_SUBLANES
POOL_ROWS = 512
QKV_ROWS, QKV_COLS = 1024, 1024
ATTN_BLOCK = V7X_MXU_DIM
ATTN_HEADS = 4
ATTN_UNROLL = 8
PROJ_ROWS = 512
FF_PAD = 22 * V7X_MXU_DIM
UP_ROWS, UP_COLS = 1024, 512
UP_TILES = FF_PAD // UP_COLS
UP_VAL_BLOCKS = UP_COLS // V7X_LANES
DOWN_STEPS = 4
DOWN_ROWS, DOWN_K = 512, FF_PAD // DOWN_STEPS

LOG2_E = math.log2(math.e)


def _params(*semantics):
    return pltpu.CompilerParams(dimension_semantics=semantics,
                                vmem_limit_bytes=V7X_VMEM_LIMIT_BYTES)


def _residual_layer_norm(x, y, g, b):
    v = DEEPNORM_ALPHA * x + y
    mu = jnp.mean(v, axis=-1, keepdims=True)
    c = v - mu
    var = jnp.mean(c * c, axis=-1, keepdims=True)
    return c * lax.rsqrt(var + LN_EPS) * g + b


def _prev_halo_index(rows):
    per_tile = rows // HALO_ROWS
    return lambda i, *_: (jnp.maximum(i * per_tile - 1, 0), 0)


def _pool_ln_kernel(x_ref, halo_ref, w_ref, scale_ref, g_ref, b_ref, of_ref, ob_ref):
    i = pl.program_id(0)
    rows = x_ref.shape[0]
    x = x_ref[...]
    halo = jnp.where(i == 0, 0.0, halo_ref[...])
    t1 = i * rows + lax.broadcasted_iota(jnp.int32, (rows, 1), 0) + 1
    ys = []
    for grp, win in enumerate(POOL_WINDOWS):
        cols = slice(grp * POOL_GROUP_DIM, (grp + 1) * POOL_GROUP_DIM)
        xg = x[:, cols]
        s = jnp.concatenate([halo[:, cols], xg], axis=0)
        span = 1
        while span < win:
            s = s + pltpu.roll(s, span, axis=0)
            span *= 2
        cnt = jnp.minimum(t1, win).astype(jnp.float32)
        pooled = s[HALO_ROWS:, :] / cnt - xg
        ys.append(jnp.dot(pooled.astype(jnp.bfloat16), w_ref[grp].astype(jnp.bfloat16),
                          preferred_element_type=jnp.float32))
    y = jnp.concatenate(ys, axis=1) * scale_ref[...]
    out = _residual_layer_norm(x, y, g_ref[...], b_ref[...])
    of_ref[...] = out
    ob_ref[...] = out.astype(jnp.bfloat16)


def _pool_ln(x, w, layer, scale, g, b):
    seq = x.shape[0]
    row = lambda i: (i, 0)
    fixed = lambda i: (0, 0)
    return pl.pallas_call(
        _pool_ln_kernel,
        grid=(seq // POOL_ROWS,),
        in_specs=[pl.BlockSpec((POOL_ROWS, D_MODEL), row),
                  pl.BlockSpec((HALO_ROWS, D_MODEL), _prev_halo_index(POOL_ROWS)),
                  pl.BlockSpec((None,) + w.shape[1:], lambda i: (layer, 0, 0, 0)),
                  pl.BlockSpec((1, D_MODEL), fixed),
                  pl.BlockSpec((1, D_MODEL), fixed),
                  pl.BlockSpec((1, D_MODEL), fixed)],
        out_specs=[pl.BlockSpec((POOL_ROWS, D_MODEL), row),
                   pl.BlockSpec((POOL_ROWS, D_MODEL), row)],
        out_shape=[jax.ShapeDtypeStruct((seq, D_MODEL), jnp.float32),
                   jax.ShapeDtypeStruct((seq, D_MODEL), jnp.bfloat16)],
        compiler_params=_params("arbitrary"),
        name="pool_ln",
    )(x, x, w, scale, g, b)


def _qkv_kernel(x_ref, w_ref, o_ref, *, q_col_tiles, q_scale):
    j = pl.program_id(0)
    h = jnp.dot(x_ref[...], w_ref[...].astype(jnp.bfloat16), preferred_element_type=jnp.float32)
    h = h * jnp.where(j < q_col_tiles, q_scale, 1.0)
    o_ref[...] = h.astype(jnp.bfloat16)


def _qkv(xb, w, layer):
    seq = xb.shape[0]
    n_out = w.shape[2]
    kern = functools.partial(_qkv_kernel, q_col_tiles=D_MODEL // QKV_COLS,
                             q_scale=HEAD_DIM ** -0.5 * LOG2_E)
    return pl.pallas_call(
        kern,
        grid=(n_out // QKV_COLS, seq // QKV_ROWS),
        in_specs=[pl.BlockSpec((QKV_ROWS, D_MODEL), lambda j, i: (i, 0)),
                  pl.BlockSpec((None, D_MODEL, QKV_COLS), lambda j, i: (layer, 0, j))],
        out_specs=pl.BlockSpec((QKV_ROWS, QKV_COLS), lambda j, i: (i, j)),
        out_shape=jax.ShapeDtypeStruct((seq, n_out), jnp.bfloat16),
        compiler_params=_params("arbitrary", "arbitrary"),
        name="qkv",
    )(xb, w)


def _neg_abs(z):
    bits = lax.bitcast_convert_type(z, jnp.uint32) | jnp.uint32(0x80000000)
    return lax.bitcast_convert_type(bits, jnp.float32)


def _attn_kernel(q_ref, k_ref, v_ref, ntri_ref, o_ref, z_ref, a_ref, acc_ref, carry_ref):
    i = pl.program_id(1)
    blk = q_ref.shape[0]
    n_heads = q_ref.shape[1] // HEAD_DIM
    ntri = ntri_ref[...]
    head_lanes = [slice(hd * HEAD_DIM, (hd + 1) * HEAD_DIM) for hd in range(n_heads)]

    def logits(hd, kb):
        start = pl.multiple_of(kb * blk, blk)
        return lax.dot_general(q_ref[:, head_lanes[hd]], k_ref[pl.ds(start, blk), head_lanes[hd]],
                               (((1,), (1,)), ((), ())),
                               preferred_element_type=jnp.float32)

    def weigh(hd, kb):
        start = pl.multiple_of(kb * blk, blk)
        acc_ref[hd] += jnp.dot(a_ref[hd], v_ref[pl.ds(start, blk), head_lanes[hd]],
                               preferred_element_type=jnp.float32)

    def shape_all(slot):
        us, pbs = [], []
        for hd in range(n_heads):
            z = z_ref[slot, hd]
            p = jnp.maximum(z, 0.0) + jnp.log2(1.0 + jnp.exp2(_neg_abs(z)))
            us.append(z - p)
            pbs.append(p)
        cs_all = jnp.dot(jnp.concatenate(pbs, axis=0), ntri,
                         preferred_element_type=jnp.float32)
        for hd in range(n_heads):
            cs = cs_all[hd * blk:(hd + 1) * blk]
            carry = carry_ref[hd]
            a_ref[hd] = jnp.exp2(us[hd] + cs + carry).astype(jnp.bfloat16)
            carry_ref[hd] = carry + cs[:, 0:1] - pbs[hd][:, 0:1]

    r = lax.broadcasted_iota(jnp.int32, (blk, blk), 0)
    c = lax.broadcasted_iota(jnp.int32, (blk, blk), 1)
    for hd in range(n_heads):
        z_ref[0, hd] = jnp.where(c < r, logits(hd, i), -jnp.inf)
        a_ref[hd] = jnp.zeros((blk, blk), jnp.bfloat16)
        acc_ref[hd] = jnp.zeros((blk, HEAD_DIM), jnp.float32)
        carry_ref[hd] = jnp.zeros((blk, 1), jnp.float32)

    def trip(t, slot):
        for hd in range(n_heads):
            weigh(hd, jnp.minimum(i - t + 1, i))
        shape_all(slot)
        for hd in range(n_heads):
            z_ref[1 - slot, hd] = logits(hd, jnp.maximum(i - t - 1, 0))

    def trips(first, count):
        for n in range(count):
            trip(first + n, n % 2)

    n_trips = i + 1
    lax.fori_loop(0, n_trips // ATTN_UNROLL,
                  lambda n, _: trips(ATTN_UNROLL * n, ATTN_UNROLL) or 0, 0)
    tail = n_trips % ATTN_UNROLL
    done = n_trips - tail
    for pairs in range(1, ATTN_UNROLL // 2):
        @pl.when(tail // 2 == pairs)
        def _():
            trips(done, 2 * pairs)

    @pl.when(tail % 2 == 1)
    def _():
        trip(i, 0)

    for hd in range(n_heads):
        weigh(hd, 0)
        o_ref[:, head_lanes[hd]] = acc_ref[hd].astype(jnp.bfloat16)


def _attention(qkv, ntri):
    seq = qkv.shape[0]
    width = ATTN_HEADS * HEAD_DIM
    groups = N_HEADS // ATTN_HEADS
    return pl.pallas_call(
        _attn_kernel,
        grid=(groups, seq // ATTN_BLOCK),
        in_specs=[pl.BlockSpec((ATTN_BLOCK, width), lambda h, i: (i, h)),
                  pl.BlockSpec((seq, width), lambda h, i: (0, groups + h)),
                  pl.BlockSpec((seq, width), lambda h, i: (0, 2 * groups + h)),
                  pl.BlockSpec((ATTN_BLOCK, ATTN_BLOCK), lambda h, i: (0, 0))],
        out_specs=pl.BlockSpec((ATTN_BLOCK, width), lambda h, i: (i, h)),
        out_shape=jax.ShapeDtypeStruct((seq, D_MODEL), jnp.bfloat16),
        scratch_shapes=[pltpu.VMEM((2, ATTN_HEADS, ATTN_BLOCK, ATTN_BLOCK), jnp.float32),
                        pltpu.VMEM((ATTN_HEADS, ATTN_BLOCK, ATTN_BLOCK), jnp.bfloat16),
                        pltpu.VMEM((ATTN_HEADS, ATTN_BLOCK, HEAD_DIM), jnp.float32),
                        pltpu.VMEM((ATTN_HEADS, ATTN_BLOCK, 1), jnp.float32)],
        compiler_params=_params("arbitrary", "arbitrary"),
        name="stickbreak_attn",
    )(qkv, qkv, qkv, ntri)


def _proj_ln_kernel(o_ref, x_ref, w_ref, g_ref, b_ref, of_ref, ob_ref):
    w = w_ref[...].astype(jnp.bfloat16)
    half = o_ref.shape[0] // 2
    outs = []
    for r in range(2):
        rows = slice(r * half, (r + 1) * half)
        y = jnp.dot(o_ref[rows, :], w, preferred_element_type=jnp.float32)
        outs.append(_residual_layer_norm(x_ref[rows, :], y, g_ref[...], b_ref[...]))
    out = jnp.concatenate(outs, axis=0)
    of_ref[...] = out
    ob_ref[...] = out.astype(jnp.bfloat16)


def _proj_ln(o, x, w, layer, g, b):
    seq = x.shape[0]
    row = lambda i: (i, 0)
    fixed = lambda i: (0, 0)
    return pl.pallas_call(
        _proj_ln_kernel,
        grid=(seq // PROJ_ROWS,),
        in_specs=[pl.BlockSpec((PROJ_ROWS, D_MODEL), row),
                  pl.BlockSpec((PROJ_ROWS, D_MODEL), row),
                  pl.BlockSpec((None, D_MODEL, D_MODEL), lambda i: (layer, 0, 0),
                               pipeline_mode=pl.Buffered(1)),
                  pl.BlockSpec((1, D_MODEL), fixed),
                  pl.BlockSpec((1, D_MODEL), fixed)],
        out_specs=[pl.BlockSpec((PROJ_ROWS, D_MODEL), row),
                   pl.BlockSpec((PROJ_ROWS, D_MODEL), row)],
        out_shape=[jax.ShapeDtypeStruct((seq, D_MODEL), jnp.float32),
                   jax.ShapeDtypeStruct((seq, D_MODEL), jnp.bfloat16)],
        compiler_params=_params("arbitrary"),
        name="proj_ln",
    )(o, x, w, g, b)


def _ffn_up_kernel(x_ref, halo_ref, wg_ref, *rest):
    wv_refs = rest[:UP_VAL_BLOCKS]
    cwg_ref, cwv_ref, cbg_ref, cbv_ref, o_ref, xe_ref = rest[UP_VAL_BLOCKS:]
    i = pl.program_id(0)

    @pl.when(pl.program_id(1) == 0)
    def _():
        xe_ref[0:HALO_ROWS, :] = jnp.where(i == 0, jnp.zeros_like(halo_ref), halo_ref[...])
        xe_ref[HALO_ROWS:, :] = x_ref[...]

    xe = xe_ref[...]

    def causal_conv(he, cw, cb):
        h1 = pltpu.roll(he, 1, axis=0)[HALO_ROWS:]
        h2 = pltpu.roll(he, 2, axis=0)[HALO_ROWS:]
        return cb + cw[0:1] * h2 + cw[1:2] * h1 + cw[2:3] * he[HALO_ROWS:]

    wg = wg_ref[...].astype(jnp.bfloat16)
    wv = jnp.concatenate([r[...].astype(jnp.bfloat16) for r in wv_refs], axis=1)
    gate = causal_conv(jnp.dot(xe, wg, preferred_element_type=jnp.float32), cwg_ref[...], cbg_ref[...])
    val = causal_conv(jnp.dot(xe, wv, preferred_element_type=jnp.float32), cwv_ref[...], cbv_ref[...])
    o_ref[...] = (gate * jax.nn.sigmoid(gate) * val).astype(jnp.bfloat16)


def _ffn_up(xb, w_up, layer, cwg, cwv, cbg, cbv):
    seq = xb.shape[0]
    col = lambda i, j: (0, j)
    val0 = D_FF // V7X_LANES
    last = 2 * D_FF // V7X_LANES - 1

    def val_index(m, i, j):
        return layer, 0, jnp.minimum(val0 + UP_VAL_BLOCKS * j + m, last)

    val_specs = [pl.BlockSpec((None, D_MODEL, V7X_LANES), functools.partial(val_index, m))
                 for m in range(UP_VAL_BLOCKS)]
    return pl.pallas_call(
        _ffn_up_kernel,
        grid=(seq // UP_ROWS, UP_TILES),
        in_specs=[pl.BlockSpec((UP_ROWS, D_MODEL), lambda i, j: (i, 0)),
                  pl.BlockSpec((HALO_ROWS, D_MODEL), _prev_halo_index(UP_ROWS)),
                  pl.BlockSpec((None, D_MODEL, UP_COLS), lambda i, j: (layer, 0, j)),
                  *val_specs,
                  pl.BlockSpec((CONV_WIDTH, UP_COLS), col),
                  pl.BlockSpec((CONV_WIDTH, UP_COLS), col),
                  pl.BlockSpec((1, UP_COLS), col),
                  pl.BlockSpec((1, UP_COLS), col)],
        out_specs=pl.BlockSpec((UP_ROWS, UP_COLS), lambda i, j: (i, j)),
        out_shape=jax.ShapeDtypeStruct((seq, FF_PAD), jnp.bfloat16),
        scratch_shapes=[pltpu.VMEM((HALO_ROWS + UP_ROWS, D_MODEL), jnp.bfloat16)],
        compiler_params=_params("arbitrary", "arbitrary"),
        name="ffn_up",
    )(xb, xb, w_up, *([w_up] * UP_VAL_BLOCKS), cwg, cwv, cbg, cbv)


def _ffn_down_ln_kernel(a_ref, x_ref, w_ref, g_ref, b_ref, of_ref, ob_ref, acc_ref):
    k = pl.program_id(1)

    @pl.when(k == 0)
    def _():
        acc_ref[...] = jnp.zeros_like(acc_ref)

    last = DOWN_STEPS - 1

    @pl.when(k < last)
    def _():
        acc_ref[...] += jnp.dot(a_ref[...], w_ref[...], preferred_element_type=jnp.float32)

    @pl.when(k == last)
    def _():
        kk = D_FF - last * DOWN_K
        w = w_ref[0:kk, :]
        half = a_ref.shape[0] // 2
        outs = []
        for r in range(2):
            rows = slice(r * half, (r + 1) * half)
            y = acc_ref[rows, :] + jnp.dot(a_ref[rows, 0:kk], w, preferred_element_type=jnp.float32)
            outs.append(_residual_layer_norm(x_ref[rows, :], y, g_ref[...], b_ref[...]))
        out = jnp.concatenate(outs, axis=0)
        of_ref[...] = out
        ob_ref[...] = out.astype(jnp.bfloat16)


def _ffn_down_ln(act, x, w, layer, g, b):
    seq = x.shape[0]
    row = lambda i, k: (i, 0)
    fixed = lambda i, k: (0, 0)
    return pl.pallas_call(
        _ffn_down_ln_kernel,
        grid=(seq // DOWN_ROWS, DOWN_STEPS),
        in_specs=[pl.BlockSpec((DOWN_ROWS, DOWN_K), lambda i, k: (i, k)),
                  pl.BlockSpec((DOWN_ROWS, D_MODEL), row),
                  pl.BlockSpec((None, DOWN_K, D_MODEL), lambda i, k: (layer, k, 0)),
                  pl.BlockSpec((1, D_MODEL), fixed),
                  pl.BlockSpec((1, D_MODEL), fixed)],
        out_specs=[pl.BlockSpec((DOWN_ROWS, D_MODEL), row),
                   pl.BlockSpec((DOWN_ROWS, D_MODEL), row)],
        out_shape=[jax.ShapeDtypeStruct((seq, D_MODEL), jnp.float32),
                   jax.ShapeDtypeStruct((seq, D_MODEL), jnp.bfloat16)],
        scratch_shapes=[pltpu.VMEM((DOWN_ROWS, D_MODEL), jnp.float32)],
        compiler_params=_params("arbitrary", "arbitrary"),
        name="ffn_down_ln",
    )(act, x, w, g, b)


def _pad_ff(a, axis):
    pad = [(0, 0)] * a.ndim
    pad[axis] = (0, FF_PAD - D_FF)
    return jnp.pad(a, pad)


def kernel(x, pool_w, pool_scale, attn_w_qkv, attn_w_o, ffn_w_up, ffn_conv_w, ffn_conv_b,
           ffn_w_down, ln_mix_g, ln_mix_b, ln_ffn_g, ln_ffn_b):
    batch, seq, _ = x.shape
    bf16 = jnp.bfloat16
    vec = lambda a: a.reshape(1, -1)
    ntri = -jnp.tril(jnp.ones((ATTN_BLOCK, ATTN_BLOCK), jnp.float32), k=-1)
    w_down = ffn_w_down.astype(bf16)

    outs = []
    for bi in range(batch):
        xf = x[bi]
        xb = None
        for i in range(DEPTH):
            j = i // 2
            g, b = vec(ln_mix_g[i]), vec(ln_mix_b[i])
            if i % 2 == 0:
                xf, xb = _pool_ln(xf, pool_w, j, vec(pool_scale[j]), g, b)
            else:
                qkv = _qkv(xb, attn_w_qkv, j)
                o = _attention(qkv, ntri)
                xf, xb = _proj_ln(o, xf, attn_w_o, j, g, b)
            cw, cb = ffn_conv_w[i], ffn_conv_b[i]
            act = _ffn_up(xb, ffn_w_up, i,
                          _pad_ff(cw[:, :D_FF], 1), _pad_ff(cw[:, D_FF:], 1),
                          _pad_ff(vec(cb[:D_FF]), 1), _pad_ff(vec(cb[D_FF:]), 1))
            xf, xb = _ffn_down_ln(act, xf, w_down, i, vec(ln_ffn_g[i]), vec(ln_ffn_b[i]))
        outs.append(xf)
    return jnp.stack(outs, axis=0)
```

```python
import functools
import math

import jax
import jax.numpy as jnp
from jax import lax
from jax.experimental import pallas as pl
from jax.experimental.pallas import tpu as pltpu

D_MODEL = 2048
DEPTH = 4
POOL_WINDOWS = (2, 4, 8, 16)
N_POOL_GROUPS = len(POOL_WINDOWS)
POOL_GROUP_DIM = D_MODEL // N_POOL_GROUPS
HEAD_DIM = 128
N_HEADS = D_MODEL // HEAD_DIM
D_FF = 5504
CONV_WIDTH = 3
LN_EPS = 1e-5
DEEPNORM_ALPHA = (2.0 * DEPTH) ** 0.25

V7X_MXU_DIM = 256
V7X_BF16_SUBLANES = 16
V7X_LANES = 128
V7X_VMEM_LIMIT_BYTES = 56 * 1024 * 1024

HALO_ROWS = V7X_BF16_SUBLANES
POOL_ROWS = 512
QKV_ROWS, QKV_COLS = 1024, 1024
ATTN_BLOCK = V7X_MXU_DIM
ATTN_HEADS = 4
ATTN_UNROLL = 8
PROJ_ROWS = 512
FF_PAD = 22 * V7X_MXU_DIM
UP_ROWS, UP_COLS = 1024, 512
UP_TILES = FF_PAD // UP_COLS
UP_VAL_BLOCKS = UP_COLS // V7X_LANES
UP_VAL_ROW_CHUNKS = 2
DOWN_STEPS = 4
DOWN_ROWS, DOWN_K = 512, FF_PAD // DOWN_STEPS

LOG2_E = math.log2(math.e)


def _params(*semantics):
    return pltpu.CompilerParams(dimension_semantics=semantics,
                                vmem_limit_bytes=V7X_VMEM_LIMIT_BYTES)


def _residual_layer_norm(x, y, g, b):
    v = DEEPNORM_ALPHA * x + y
    mu = jnp.mean(v, axis=-1, keepdims=True)
    c = v - mu
    var = jnp.mean(c * c, axis=-1, keepdims=True)
    return c * lax.rsqrt(var + LN_EPS) * g + b


def _prev_halo_index(rows):
    per_tile = rows // HALO_ROWS
    return lambda i, *_: (jnp.maximum(i * per_tile - 1, 0), 0)


def _pool_ln_kernel(x_ref, halo_ref, w_ref, scale_ref, g_ref, b_ref, of_ref, ob_ref):
    i = pl.program_id(0)
    rows = x_ref.shape[0]
    x = x_ref[...]
    halo = jnp.where(i == 0, 0.0, halo_ref[...])
    t1 = i * rows + lax.broadcasted_iota(jnp.int32, (rows, 1), 0) + 1
    ys = []
    for grp, win in enumerate(POOL_WINDOWS):
        cols = slice(grp * POOL_GROUP_DIM, (grp + 1) * POOL_GROUP_DIM)
        xg = x[:, cols]
        s = jnp.concatenate([halo[:, cols], xg], axis=0)
        span = 1
        while span < win:
            s = s + pltpu.roll(s, span, axis=0)
            span *= 2
        cnt = jnp.minimum(t1, win).astype(jnp.float32)
        pooled = s[HALO_ROWS:, :] / cnt - xg
        ys.append(jnp.dot(pooled.astype(jnp.bfloat16), w_ref[grp].astype(jnp.bfloat16),
                          preferred_element_type=jnp.float32))
    y = jnp.concatenate(ys, axis=1) * scale_ref[...]
    out = _residual_layer_norm(x, y, g_ref[...], b_ref[...])
    of_ref[...] = out
    ob_ref[...] = out.astype(jnp.bfloat16)


def _pool_ln(x, w, layer, scale, g, b):
    seq = x.shape[0]
    row = lambda i: (i, 0)
    fixed = lambda i: (0, 0)
    return pl.pallas_call(
        _pool_ln_kernel,
        grid=(seq // POOL_ROWS,),
        in_specs=[pl.BlockSpec((POOL_ROWS, D_MODEL), row),
                  pl.BlockSpec((HALO_ROWS, D_MODEL), _prev_halo_index(POOL_ROWS)),
                  pl.BlockSpec((None,) + w.shape[1:], lambda i: (layer, 0, 0, 0)),
                  pl.BlockSpec((1, D_MODEL), fixed),
                  pl.BlockSpec((1, D_MODEL), fixed),
                  pl.BlockSpec((1, D_MODEL), fixed)],
        out_specs=[pl.BlockSpec((POOL_ROWS, D_MODEL), row),
                   pl.BlockSpec((POOL_ROWS, D_MODEL), row)],
        out_shape=[jax.ShapeDtypeStruct((seq, D_MODEL), jnp.float32),
                   jax.ShapeDtypeStruct((seq, D_MODEL), jnp.bfloat16)],
        compiler_params=_params("arbitrary"),
        name="pool_ln",
    )(x, x, w, scale, g, b)


def _qkv_kernel(x_ref, w_ref, o_ref, *, q_col_tiles, q_scale):
    j = pl.program_id(0)
    h = jnp.dot(x_ref[...], w_ref[...].astype(jnp.bfloat16), preferred_element_type=jnp.float32)
    h = h * jnp.where(j < q_col_tiles, q_scale, 1.0)
    o_ref[...] = h.astype(jnp.bfloat16)


def _qkv(xb, w, layer):
    seq = xb.shape[0]
    n_out = w.shape[2]
    kern = functools.partial(_qkv_kernel, q_col_tiles=D_MODEL // QKV_COLS,
                             q_scale=HEAD_DIM ** -0.5 * LOG2_E)
    return pl.pallas_call(
        kern,
        grid=(n_out // QKV_COLS, seq // QKV_ROWS),
        in_specs=[pl.BlockSpec((QKV_ROWS, D_MODEL), lambda j, i: (i, 0)),
                  pl.BlockSpec((None, D_MODEL, QKV_COLS), lambda j, i: (layer, 0, j))],
        out_specs=pl.BlockSpec((QKV_ROWS, QKV_COLS), lambda j, i: (i, j)),
        out_shape=jax.ShapeDtypeStruct((seq, n_out), jnp.bfloat16),
        compiler_params=_params("arbitrary", "arbitrary"),
        name="qkv",
    )(xb, w)


def _neg_abs(z):
    bits = lax.bitcast_convert_type(z, jnp.uint32) | jnp.uint32(0x80000000)
    return lax.bitcast_convert_type(bits, jnp.float32)


def _attn_kernel(q_ref, k_ref, v_ref, ntri_ref, o_ref, z_ref, a_ref, acc_ref, carry_ref):
    i = pl.program_id(1)
    blk = q_ref.shape[0]
    n_heads = q_ref.shape[1] // HEAD_DIM
    ntri = ntri_ref[...]
    head_lanes = [slice(hd * HEAD_DIM, (hd + 1) * HEAD_DIM) for hd in range(n_heads)]

    def logits(hd, kb):
        start = pl.multiple_of(kb * blk, blk)
        return lax.dot_general(q_ref[:, head_lanes[hd]], k_ref[pl.ds(start, blk), head_lanes[hd]],
                               (((1,), (1,)), ((), ())),
                               preferred_element_type=jnp.float32)

    def weigh(hd, kb):
        start = pl.multiple_of(kb * blk, blk)
        acc_ref[hd] += jnp.dot(a_ref[hd], v_ref[pl.ds(start, blk), head_lanes[hd]],
                               preferred_element_type=jnp.float32)

    def shape_all(slot):
        us, pbs = [], []
        for hd in range(n_heads):
            z = z_ref[slot, hd]
            p = jnp.maximum(z, 0.0) + jnp.log2(1.0 + jnp.exp2(_neg_abs(z)))
            us.append(z - p)
            pbs.append(p)
        cs_all = jnp.dot(jnp.concatenate(pbs, axis=0), ntri,
                         preferred_element_type=jnp.float32)
        for hd in range(n_heads):
            cs = cs_all[hd * blk:(hd + 1) * blk]
            carry = carry_ref[hd]
            a_ref[hd] = jnp.exp2(us[hd] + cs + carry).astype(jnp.bfloat16)
            carry_ref[hd] = carry + cs[:, 0:1] - pbs[hd][:, 0:1]

    r = lax.broadcasted_iota(jnp.int32, (blk, blk), 0)
    c = lax.broadcasted_iota(jnp.int32, (blk, blk), 1)
    for hd in range(n_heads):
        z_ref[0, hd] = jnp.where(c < r, logits(hd, i), -jnp.inf)
        a_ref[hd] = jnp.zeros((blk, blk), jnp.bfloat16)
        acc_ref[hd] = jnp.zeros((blk, HEAD_DIM), jnp.float32)
        carry_ref[hd] = jnp.zeros((blk, 1), jnp.float32)

    def trip(t, slot):
        for hd in range(n_heads):
            weigh(hd, jnp.minimum(i - t + 1, i))
        shape_all(slot)
        for hd in range(n_heads):
            z_ref[1 - slot, hd] = logits(hd, jnp.maximum(i - t - 1, 0))

    def trips(first, count):
        for n in range(count):
            trip(first + n, n % 2)

    n_trips = i + 1
    lax.fori_loop(0, n_trips // ATTN_UNROLL,
                  lambda n, _: trips(ATTN_UNROLL * n, ATTN_UNROLL) or 0, 0)
    tail = n_trips % ATTN_UNROLL
    done = n_trips - tail
    for pairs in range(1, ATTN_UNROLL // 2):
        @pl.when(tail // 2 == pairs)
        def _():
            trips(done, 2 * pairs)

    @pl.when(tail % 2 == 1)
    def _():
        trip(i, 0)

    for hd in range(n_heads):
        weigh(hd, 0)
        o_ref[:, head_lanes[hd]] = acc_ref[hd].astype(jnp.bfloat16)


def _attention(qkv, ntri):
    seq = qkv.shape[0]
    width = ATTN_HEADS * HEAD_DIM
    groups = N_HEADS // ATTN_HEADS
    return pl.pallas_call(
        _attn_kernel,
        grid=(groups, seq // ATTN_BLOCK),
        in_specs=[pl.BlockSpec((ATTN_BLOCK, width), lambda h, i: (i, h)),
                  pl.BlockSpec((seq, width), lambda h, i: (0, groups + h)),
                  pl.BlockSpec((seq, width), lambda h, i: (0, 2 * groups + h)),
                  pl.BlockSpec((ATTN_BLOCK, ATTN_BLOCK), lambda h, i: (0, 0))],
        out_specs=pl.BlockSpec((ATTN_BLOCK, width), lambda h, i: (i, h)),
        out_shape=jax.ShapeDtypeStruct((seq, D_MODEL), jnp.bfloat16),
        scratch_shapes=[pltpu.VMEM((2, ATTN_HEADS, ATTN_BLOCK, ATTN_BLOCK), jnp.float32),
                        pltpu.VMEM((ATTN_HEADS, ATTN_BLOCK, ATTN_BLOCK), jnp.bfloat16),
                        pltpu.VMEM((ATTN_HEADS, ATTN_BLOCK, HEAD_DIM), jnp.float32),
                        pltpu.VMEM((ATTN_HEADS, ATTN_BLOCK, 1), jnp.float32)],
        compiler_params=_params("arbitrary", "arbitrary"),
        name="stickbreak_attn",
    )(qkv, qkv, qkv, ntri)


def _proj_ln_kernel(o_ref, x_ref, w_ref, g_ref, b_ref, of_ref, ob_ref):
    w = w_ref[...].astype(jnp.bfloat16)
    half = o_ref.shape[0] // 2
    outs = []
    for r in range(2):
        rows = slice(r * half, (r + 1) * half)
        y = jnp.dot(o_ref[rows, :], w, preferred_element_type=jnp.float32)
        outs.append(_residual_layer_norm(x_ref[rows, :], y, g_ref[...], b_ref[...]))
    out = jnp.concatenate(outs, axis=0)
    of_ref[...] = out
    ob_ref[...] = out.astype(jnp.bfloat16)


def _proj_ln(o, x, w, layer, g, b):
    seq = x.shape[0]
    row = lambda i: (i, 0)
    fixed = lambda i: (0, 0)
    return pl.pallas_call(
        _proj_ln_kernel,
        grid=(seq // PROJ_ROWS,),
        in_specs=[pl.BlockSpec((PROJ_ROWS, D_MODEL), row),
                  pl.BlockSpec((PROJ_ROWS, D_MODEL), row),
                  pl.BlockSpec((None, D_MODEL, D_MODEL), lambda i: (layer, 0, 0),
                               pipeline_mode=pl.Buffered(1)),
                  pl.BlockSpec((1, D_MODEL), fixed),
                  pl.BlockSpec((1, D_MODEL), fixed)],
        out_specs=[pl.BlockSpec((PROJ_ROWS, D_MODEL), row),
                   pl.BlockSpec((PROJ_ROWS, D_MODEL), row)],
        out_shape=[jax.ShapeDtypeStruct((seq, D_MODEL), jnp.float32),
                   jax.ShapeDtypeStruct((seq, D_MODEL), jnp.bfloat16)],
        compiler_params=_params("arbitrary"),
        name="proj_ln",
    )(o, x, w, g, b)


def _ffn_up_kernel(x_ref, halo_ref, wg_ref, *rest):
    wv_refs = rest[:UP_VAL_BLOCKS]
    cwg_ref, cwv_ref, cbg_ref, cbv_ref, wd_ref, o_ref, wdb_ref, xe_ref = rest[UP_VAL_BLOCKS:]
    i = pl.program_id(0)

    @pl.when(pl.program_id(1) == 0)
    def _():
        xe_ref[0:HALO_ROWS, :] = jnp.where(i == 0, jnp.zeros_like(halo_ref), halo_ref[...])
        xe_ref[HALO_ROWS:, :] = x_ref[...]

    xe = xe_ref[...]

    def causal_conv(he, cw, cb):
        h1 = pltpu.roll(he, 1, axis=0)[HALO_ROWS:]
        h2 = pltpu.roll(he, 2, axis=0)[HALO_ROWS:]
        return cb + cw[0:1] * h2 + cw[1:2] * h1 + cw[2:3] * he[HALO_ROWS:]

    wg = wg_ref[...].astype(jnp.bfloat16)
    wv = jnp.concatenate([r[...].astype(jnp.bfloat16) for r in wv_refs], axis=1)
    gate = causal_conv(jnp.dot(xe, wg, preferred_element_type=jnp.float32), cwg_ref[...], cbg_ref[...])
    gate = gate * jax.nn.sigmoid(gate)
    half = o_ref.shape[0] // UP_VAL_ROW_CHUNKS
    acts = []
    for r in range(UP_VAL_ROW_CHUNKS):
        rows = slice(r * half, (r + 1) * half)
        hv = jnp.dot(xe_ref[r * half:(r + 1) * half + HALO_ROWS, :], wv,
                     preferred_element_type=jnp.float32)
        val = causal_conv(hv, cwv_ref[...], cbv_ref[...])
        acts.append((gate[rows] * val).astype(jnp.bfloat16))
    o_ref[...] = jnp.concatenate(acts, axis=0)
    wdb_ref[...] = wd_ref[...].astype(jnp.bfloat16)


def _ffn_up(xb, w_up, w_down, layer, cwg, cwv, cbg, cbv):
    seq = xb.shape[0]
    col = lambda i, j: (0, j)
    wd_cols = D_MODEL // (seq // UP_ROWS)
    val0 = D_FF // V7X_LANES
    last = 2 * D_FF // V7X_LANES - 1

    def val_index(m, i, j):
        return layer, 0, jnp.minimum(val0 + UP_VAL_BLOCKS * j + m, last)

    val_specs = [pl.BlockSpec((None, D_MODEL, V7X_LANES), functools.partial(val_index, m))
                 for m in range(UP_VAL_BLOCKS)]
    return pl.pallas_call(
        _ffn_up_kernel,
        grid=(seq // UP_ROWS, UP_TILES),
        in_specs=[pl.BlockSpec((UP_ROWS, D_MODEL), lambda i, j: (i, 0)),
                  pl.BlockSpec((HALO_ROWS, D_MODEL), _prev_halo_index(UP_ROWS)),
                  pl.BlockSpec((None, D_MODEL, UP_COLS), lambda i, j: (layer, 0, j)),
                  *val_specs,
                  pl.BlockSpec((CONV_WIDTH, UP_COLS), col),
                  pl.BlockSpec((CONV_WIDTH, UP_COLS), col),
                  pl.BlockSpec((1, UP_COLS), col),
                  pl.BlockSpec((1, UP_COLS), col),
                  pl.BlockSpec((None, UP_COLS, wd_cols), lambda i, j: (layer, j, i))],
        out_specs=[pl.BlockSpec((UP_ROWS, UP_COLS), lambda i, j: (i, j)),
                   pl.BlockSpec((UP_COLS, wd_cols), lambda i, j: (j, i))],
        out_shape=[jax.ShapeDtypeStruct((seq, FF_PAD), jnp.bfloat16),
                   jax.ShapeDtypeStruct((D_FF, D_MODEL), jnp.bfloat16)],
        scratch_shapes=[pltpu.VMEM((HALO_ROWS + UP_ROWS, D_MODEL), jnp.bfloat16)],
        compiler_params=_params("arbitrary", "arbitrary"),
        name="ffn_up",
    )(xb, xb, w_up, *([w_up] * UP_VAL_BLOCKS), cwg, cwv, cbg, cbv, w_down)


def _ffn_down_ln_kernel(a_ref, x_ref, w_ref, g_ref, b_ref, of_ref, ob_ref, acc_ref):
    k = pl.program_id(1)

    @pl.when(k == 0)
    def _():
        acc_ref[...] = jnp.zeros_like(acc_ref)

    last = DOWN_STEPS - 1

    @pl.when(k < last)
    def _():
        acc_ref[...] += jnp.dot(a_ref[...], w_ref[...], preferred_element_type=jnp.float32)

    @pl.when(k == last)
    def _():
        kk = D_FF - last * DOWN_K
        w = w_ref[0:kk, :]
        half = a_ref.shape[0] // 2
        outs = []
        for r in range(2):
            rows = slice(r * half, (r + 1) * half)
            y = acc_ref[rows, :] + jnp.dot(a_ref[rows, 0:kk], w, preferred_element_type=jnp.float32)
            outs.append(_residual_layer_norm(x_ref[rows, :], y, g_ref[...], b_ref[...]))
        out = jnp.concatenate(outs, axis=0)
        of_ref[...] = out
        ob_ref[...] = out.astype(jnp.bfloat16)


def _ffn_down_ln(act, x, w, g, b):
    seq = x.shape[0]
    row = lambda i, k: (i, 0)
    fixed = lambda i, k: (0, 0)
    return pl.pallas_call(
        _ffn_down_ln_kernel,
        grid=(seq // DOWN_ROWS, DOWN_STEPS),
        in_specs=[pl.BlockSpec((DOWN_ROWS, DOWN_K), lambda i, k: (i, k)),
                  pl.BlockSpec((DOWN_ROWS, D_MODEL), row),
                  pl.BlockSpec((DOWN_K, D_MODEL), lambda i, k: (k, 0)),
                  pl.BlockSpec((1, D_MODEL), fixed),
                  pl.BlockSpec((1, D_MODEL), fixed)],
        out_specs=[pl.BlockSpec((DOWN_ROWS, D_MODEL), row),
                   pl.BlockSpec((DOWN_ROWS, D_MODEL), row)],
        out_shape=[jax.ShapeDtypeStruct((seq, D_MODEL), jnp.float32),
                   jax.ShapeDtypeStruct((seq, D_MODEL), jnp.bfloat16)],
        scratch_shapes=[pltpu.VMEM((DOWN_ROWS, D_MODEL), jnp.float32)],
        compiler_params=_params("arbitrary", "arbitrary"),
        name="ffn_down_ln",
    )(act, x, w, g, b)


def _pad_ff(a, axis):
    pad = [(0, 0)] * a.ndim
    pad[axis] = (0, FF_PAD - D_FF)
    return jnp.pad(a, pad)


def kernel(x, pool_w, pool_scale, attn_w_qkv, attn_w_o, ffn_w_up, ffn_conv_w, ffn_conv_b,
           ffn_w_down, ln_mix_g, ln_mix_b, ln_ffn_g, ln_ffn_b):
    batch, seq, _ = x.shape
    bf16 = jnp.bfloat16
    vec = lambda a: a.reshape(1, -1)
    ntri = -jnp.tril(jnp.ones((ATTN_BLOCK, ATTN_BLOCK), jnp.float32), k=-1)

    outs = []
    for bi in range(batch):
        xf = x[bi]
        xb = None
        for i in range(DEPTH):
            j = i // 2
            g, b = vec(ln_mix_g[i]), vec(ln_mix_b[i])
            if i % 2 == 0:
                xf, xb = _pool_ln(xf, pool_w, j, vec(pool_scale[j]), g, b)
            else:
                qkv = _qkv(xb, attn_w_qkv, j)
                o = _attention(qkv, ntri)
                xf, xb = _proj_ln(o, xf, attn_w_o, j, g, b)
            cw, cb = ffn_conv_w[i], ffn_conv_b[i]
            act, w_down = _ffn_up(xb, ffn_w_up, ffn_w_down, i,
                          _pad_ff(cw[:, :D_FF], 1), _pad_ff(cw[:, D_FF:], 1),
                          _pad_ff(vec(cb[:D_FF]), 1), _pad_ff(vec(cb[D_FF:]), 1))
            xf, xb = _ffn_down_ln(act, xf, w_down, vec(ln_ffn_g[i]), vec(ln_ffn_b[i]))
        outs.append(xf)
    return jnp.stack(outs, axis=0)
```

```python
import functools
import math

import jax
import jax.numpy as jnp
from jax import lax
from jax.experimental import pallas as pl
from jax.experimental.pallas import tpu as pltpu

D_MODEL = 2048
DEPTH = 4
POOL_WINDOWS = (2, 4, 8, 16)
N_POOL_GROUPS = len(POOL_WINDOWS)
POOL_GROUP_DIM = D_MODEL // N_POOL_GROUPS
HEAD_DIM = 128
N_HEADS = D_MODEL // HEAD_DIM
D_FF = 5504
CONV_WIDTH = 3
LN_EPS = 1e-5
DEEPNORM_ALPHA = (2.0 * DEPTH) ** 0.25

V7X_MXU_DIM = 256
V7X_BF16_SUBLANES = 16
V7X_LANES = 128
V7X_VMEM_LIMIT_BYTES = 60 * 1024 * 1024

HALO_ROWS = V7X_BF16_SUBLANES
POOL_ROWS = 512
QKV_ROWS, QKV_COLS = 1024, 1024
ATTN_BLOCK = V7X_MXU_DIM
ATTN_HEADS = 4
ATTN_UNROLL = 8
PROJ_ROWS = 512
FF_PAD = 22 * V7X_MXU_DIM
UP_ROWS, UP_COLS = 1024, 512
UP_TILES = FF_PAD // UP_COLS
UP_VAL_BLOCKS = UP_COLS // V7X_LANES
UP_VAL_ROW_SPLITS = (512,)
DOWN_ROWS, DOWN_K = 1024, 1024
DOWN_STEPS = -(-D_FF // DOWN_K)

LOG2_E = math.log2(math.e)


def _params(*semantics):
    return pltpu.CompilerParams(dimension_semantics=semantics,
                                vmem_limit_bytes=V7X_VMEM_LIMIT_BYTES)


def _residual_layer_norm(x, y, g, b):
    v = DEEPNORM_ALPHA * x + y
    mu = jnp.mean(v, axis=-1, keepdims=True)
    c = v - mu
    var = jnp.mean(c * c, axis=-1, keepdims=True)
    return c * lax.rsqrt(var + LN_EPS) * g + b


def _prev_halo_index(rows):
    per_tile = rows // HALO_ROWS
    return lambda i, *_: (jnp.maximum(i * per_tile - 1, 0), 0)


def _pool_ln_kernel(x_ref, halo_ref, w_ref, scale_ref, g_ref, b_ref, of_ref, ob_ref):
    i = pl.program_id(0)
    rows = x_ref.shape[0]
    x = x_ref[...]
    halo = jnp.where(i == 0, 0.0, halo_ref[...])
    t1 = i * rows + lax.broadcasted_iota(jnp.int32, (rows, 1), 0) + 1
    ys = []
    for grp, win in enumerate(POOL_WINDOWS):
        cols = slice(grp * POOL_GROUP_DIM, (grp + 1) * POOL_GROUP_DIM)
        xg = x[:, cols]
        s = jnp.concatenate([halo[:, cols], xg], axis=0)
        span = 1
        while span < win:
            s = s + pltpu.roll(s, span, axis=0)
            span *= 2
        cnt = jnp.minimum(t1, win).astype(jnp.float32)
        pooled = s[HALO_ROWS:, :] / cnt - xg
        ys.append(jnp.dot(pooled.astype(jnp.bfloat16), w_ref[grp].astype(jnp.bfloat16),
                          preferred_element_type=jnp.float32))
    y = jnp.concatenate(ys, axis=1) * scale_ref[...]
    out = _residual_layer_norm(x, y, g_ref[...], b_ref[...])
    of_ref[...] = out
    ob_ref[...] = out.astype(jnp.bfloat16)


def _pool_ln(x, w, layer, scale, g, b):
    seq = x.shape[0]
    row = lambda i: (i, 0)
    fixed = lambda i: (0, 0)
    return pl.pallas_call(
        _pool_ln_kernel,
        grid=(seq // POOL_ROWS,),
        in_specs=[pl.BlockSpec((POOL_ROWS, D_MODEL), row),
                  pl.BlockSpec((HALO_ROWS, D_MODEL), _prev_halo_index(POOL_ROWS)),
                  pl.BlockSpec((None,) + w.shape[1:], lambda i: (layer, 0, 0, 0)),
                  pl.BlockSpec((1, D_MODEL), fixed),
                  pl.BlockSpec((1, D_MODEL), fixed),
                  pl.BlockSpec((1, D_MODEL), fixed)],
        out_specs=[pl.BlockSpec((POOL_ROWS, D_MODEL), row),
                   pl.BlockSpec((POOL_ROWS, D_MODEL), row)],
        out_shape=[jax.ShapeDtypeStruct((seq, D_MODEL), jnp.float32),
                   jax.ShapeDtypeStruct((seq, D_MODEL), jnp.bfloat16)],
        compiler_params=_params("arbitrary"),
        name="pool_ln",
    )(x, x, w, scale, g, b)


def _qkv_kernel(x_ref, w_ref, o_ref, *, q_col_tiles, q_scale):
    j = pl.program_id(0)
    h = jnp.dot(x_ref[...], w_ref[...].astype(jnp.bfloat16), preferred_element_type=jnp.float32)
    h = h * jnp.where(j < q_col_tiles, q_scale, 1.0)
    o_ref[...] = h.astype(jnp.bfloat16)


def _qkv(xb, w, layer):
    seq = xb.shape[0]
    n_out = w.shape[2]
    kern = functools.partial(_qkv_kernel, q_col_tiles=D_MODEL // QKV_COLS,
                             q_scale=HEAD_DIM ** -0.5 * LOG2_E)
    return pl.pallas_call(
        kern,
        grid=(n_out // QKV_COLS, seq // QKV_ROWS),
        in_specs=[pl.BlockSpec((QKV_ROWS, D_MODEL), lambda j, i: (i, 0)),
                  pl.BlockSpec((None, D_MODEL, QKV_COLS), lambda j, i: (layer, 0, j))],
        out_specs=pl.BlockSpec((QKV_ROWS, QKV_COLS), lambda j, i: (i, j)),
        out_shape=jax.ShapeDtypeStruct((seq, n_out), jnp.bfloat16),
        compiler_params=_params("arbitrary", "arbitrary"),
        name="qkv",
    )(xb, w)


def _neg_abs(z):
    bits = lax.bitcast_convert_type(z, jnp.uint32) | jnp.uint32(0x80000000)
    return lax.bitcast_convert_type(bits, jnp.float32)


def _attn_kernel(q_ref, k_ref, v_ref, ntri_ref, o_ref, z_ref, a_ref, acc_ref, carry_ref):
    i = pl.program_id(1)
    blk = q_ref.shape[0]
    n_heads = q_ref.shape[1] // HEAD_DIM
    ntri = ntri_ref[...]
    head_lanes = [slice(hd * HEAD_DIM, (hd + 1) * HEAD_DIM) for hd in range(n_heads)]

    def logits(hd, kb):
        start = pl.multiple_of(kb * blk, blk)
        return lax.dot_general(q_ref[:, head_lanes[hd]], k_ref[pl.ds(start, blk), head_lanes[hd]],
                               (((1,), (1,)), ((), ())),
                               preferred_element_type=jnp.float32)

    def weigh(hd, kb):
        start = pl.multiple_of(kb * blk, blk)
        acc_ref[hd] += jnp.dot(a_ref[hd], v_ref[pl.ds(start, blk), head_lanes[hd]],
                               preferred_element_type=jnp.float32)

    def shape_all(slot):
        us, pbs = [], []
        for hd in range(n_heads):
            z = z_ref[slot, hd]
            p = jnp.maximum(z, 0.0) + jnp.log2(1.0 + jnp.exp2(_neg_abs(z)))
            us.append(z - p)
            pbs.append(p)
        cs_all = jnp.dot(jnp.concatenate(pbs, axis=0), ntri,
                         preferred_element_type=jnp.float32)
        for hd in range(n_heads):
            cs = cs_all[hd * blk:(hd + 1) * blk]
            carry = carry_ref[hd]
            a_ref[hd] = jnp.exp2(us[hd] + cs + carry).astype(jnp.bfloat16)
            carry_ref[hd] = carry + cs[:, 0:1] - pbs[hd][:, 0:1]

    r = lax.broadcasted_iota(jnp.int32, (blk, blk), 0)
    c = lax.broadcasted_iota(jnp.int32, (blk, blk), 1)
    for hd in range(n_heads):
        z_ref[0, hd] = jnp.where(c < r, logits(hd, i), -jnp.inf)
        a_ref[hd] = jnp.zeros((blk, blk), jnp.bfloat16)
        acc_ref[hd] = jnp.zeros((blk, HEAD_DIM), jnp.float32)
        carry_ref[hd] = jnp.zeros((blk, 1), jnp.float32)

    def trip(t, slot):
        for hd in range(n_heads):
            weigh(hd, jnp.minimum(i - t + 1, i))
        shape_all(slot)
        for hd in range(n_heads):
            z_ref[1 - slot, hd] = logits(hd, jnp.maximum(i - t - 1, 0))

    def trips(first, count):
        for n in range(count):
            trip(first + n, n % 2)

    n_trips = i + 1
    lax.fori_loop(0, n_trips // ATTN_UNROLL,
                  lambda n, _: trips(ATTN_UNROLL * n, ATTN_UNROLL) or 0, 0)
    tail = n_trips % ATTN_UNROLL
    done = n_trips - tail
    for pairs in range(1, ATTN_UNROLL // 2):
        @pl.when(tail // 2 == pairs)
        def _():
            trips(done, 2 * pairs)

    @pl.when(tail % 2 == 1)
    def _():
        trip(i, 0)

    for hd in range(n_heads):
        weigh(hd, 0)
        o_ref[:, head_lanes[hd]] = acc_ref[hd].astype(jnp.bfloat16)


def _attention(qkv, ntri):
    seq = qkv.shape[0]
    width = ATTN_HEADS * HEAD_DIM
    groups = N_HEADS // ATTN_HEADS
    return pl.pallas_call(
        _attn_kernel,
        grid=(groups, seq // ATTN_BLOCK),
        in_specs=[pl.BlockSpec((ATTN_BLOCK, width), lambda h, i: (i, h)),
                  pl.BlockSpec((seq, width), lambda h, i: (0, groups + h)),
                  pl.BlockSpec((seq, width), lambda h, i: (0, 2 * groups + h)),
                  pl.BlockSpec((ATTN_BLOCK, ATTN_BLOCK), lambda h, i: (0, 0))],
        out_specs=pl.BlockSpec((ATTN_BLOCK, width), lambda h, i: (i, h)),
        out_shape=jax.ShapeDtypeStruct((seq, D_MODEL), jnp.bfloat16),
        scratch_shapes=[pltpu.VMEM((2, ATTN_HEADS, ATTN_BLOCK, ATTN_BLOCK), jnp.float32),
                        pltpu.VMEM((ATTN_HEADS, ATTN_BLOCK, ATTN_BLOCK), jnp.bfloat16),
                        pltpu.VMEM((ATTN_HEADS, ATTN_BLOCK, HEAD_DIM), jnp.float32),
                        pltpu.VMEM((ATTN_HEADS, ATTN_BLOCK, 1), jnp.float32)],
        compiler_params=_params("arbitrary", "arbitrary"),
        name="stickbreak_attn",
    )(qkv, qkv, qkv, ntri)


def _proj_ln_kernel(o_ref, x_ref, w_ref, g_ref, b_ref, of_ref, ob_ref):
    w = w_ref[...].astype(jnp.bfloat16)
    half = o_ref.shape[0] // 2
    outs = []
    for r in range(2):
        rows = slice(r * half, (r + 1) * half)
        y = jnp.dot(o_ref[rows, :], w, preferred_element_type=jnp.float32)
        outs.append(_residual_layer_norm(x_ref[rows, :], y, g_ref[...], b_ref[...]))
    out = jnp.concatenate(outs, axis=0)
    of_ref[...] = out
    ob_ref[...] = out.astype(jnp.bfloat16)


def _proj_ln(o, x, w, layer, g, b):
    seq = x.shape[0]
    row = lambda i: (i, 0)
    fixed = lambda i: (0, 0)
    return pl.pallas_call(
        _proj_ln_kernel,
        grid=(seq // PROJ_ROWS,),
        in_specs=[pl.BlockSpec((PROJ_ROWS, D_MODEL), row),
                  pl.BlockSpec((PROJ_ROWS, D_MODEL), row),
                  pl.BlockSpec((None, D_MODEL, D_MODEL), lambda i: (layer, 0, 0),
                               pipeline_mode=pl.Buffered(1)),
                  pl.BlockSpec((1, D_MODEL), fixed),
                  pl.BlockSpec((1, D_MODEL), fixed)],
        out_specs=[pl.BlockSpec((PROJ_ROWS, D_MODEL), row),
                   pl.BlockSpec((PROJ_ROWS, D_MODEL), row)],
        out_shape=[jax.ShapeDtypeStruct((seq, D_MODEL), jnp.float32),
                   jax.ShapeDtypeStruct((seq, D_MODEL), jnp.bfloat16)],
        compiler_params=_params("arbitrary"),
        name="proj_ln",
    )(o, x, w, g, b)


def _ffn_up_kernel(x_ref, halo_ref, wg_ref, *rest):
    wv_refs = rest[:UP_VAL_BLOCKS]
    cwg_ref, cwv_ref, cbg_ref, cbv_ref, wd_ref, o_ref, wdb_ref, xe_ref = rest[UP_VAL_BLOCKS:]
    i = pl.program_id(0)

    @pl.when(pl.program_id(1) == 0)
    def _():
        xe_ref[0:HALO_ROWS, :] = jnp.where(i == 0, jnp.zeros_like(halo_ref), halo_ref[...])
        xe_ref[HALO_ROWS:, :] = x_ref[...]

    xe = xe_ref[...]

    def causal_conv(he, cw, cb):
        h1 = pltpu.roll(he, 1, axis=0)[HALO_ROWS:]
        h2 = pltpu.roll(he, 2, axis=0)[HALO_ROWS:]
        return cb + cw[0:1] * h2 + cw[1:2] * h1 + cw[2:3] * he[HALO_ROWS:]

    wg = wg_ref[...].astype(jnp.bfloat16)
    wv = jnp.concatenate([r[...].astype(jnp.bfloat16) for r in wv_refs], axis=1)
    gate = causal_conv(jnp.dot(xe, wg, preferred_element_type=jnp.float32), cwg_ref[...], cbg_ref[...])
    gate = gate * jax.nn.sigmoid(gate)
    acts = []
    for start, stop in zip((0,) + UP_VAL_ROW_SPLITS, UP_VAL_ROW_SPLITS + (o_ref.shape[0],)):
        rows = slice(start, stop)
        hv = jnp.dot(xe_ref[start:stop + HALO_ROWS, :], wv, preferred_element_type=jnp.float32)
        val = causal_conv(hv, cwv_ref[...], cbv_ref[...])
        acts.append((gate[rows] * val).astype(jnp.bfloat16))
    o_ref[...] = jnp.concatenate(acts, axis=0)
    wdb_ref[...] = wd_ref[...].astype(jnp.bfloat16)


def _ffn_up(xb, w_up, w_down, layer, cwg, cwv, cbg, cbv):
    seq = xb.shape[0]
    col = lambda i, j: (0, j)
    wd_cols = D_MODEL // (seq // UP_ROWS)
    val0 = D_FF // V7X_LANES
    last = 2 * D_FF // V7X_LANES - 1

    def val_index(m, i, j):
        return layer, 0, jnp.minimum(val0 + UP_VAL_BLOCKS * j + m, last)

    val_specs = [pl.BlockSpec((None, D_MODEL, V7X_LANES), functools.partial(val_index, m))
                 for m in range(UP_VAL_BLOCKS)]
    return pl.pallas_call(
        _ffn_up_kernel,
        grid=(seq // UP_ROWS, UP_TILES),
        in_specs=[pl.BlockSpec((UP_ROWS, D_MODEL), lambda i, j: (i, 0)),
                  pl.BlockSpec((HALO_ROWS, D_MODEL), _prev_halo_index(UP_ROWS)),
                  pl.BlockSpec((None, D_MODEL, UP_COLS), lambda i, j: (layer, 0, j)),
                  *val_specs,
                  pl.BlockSpec((CONV_WIDTH, UP_COLS), col),
                  pl.BlockSpec((CONV_WIDTH, UP_COLS), col),
                  pl.BlockSpec((1, UP_COLS), col),
                  pl.BlockSpec((1, UP_COLS), col),
                  pl.BlockSpec((None, UP_COLS, wd_cols), lambda i, j: (layer, j, i))],
        out_specs=[pl.BlockSpec((UP_ROWS, UP_COLS), lambda i, j: (i, j)),
                   pl.BlockSpec((UP_COLS, wd_cols), lambda i, j: (j, i))],
        out_shape=[jax.ShapeDtypeStruct((seq, FF_PAD), jnp.bfloat16),
                   jax.ShapeDtypeStruct((D_FF, D_MODEL), jnp.bfloat16)],
        scratch_shapes=[pltpu.VMEM((HALO_ROWS + UP_ROWS, D_MODEL), jnp.bfloat16)],
        compiler_params=_params("arbitrary", "arbitrary"),
        name="ffn_up",
    )(xb, xb, w_up, *([w_up] * UP_VAL_BLOCKS), cwg, cwv, cbg, cbv, w_down)


def _ffn_down_ln_kernel(a_ref, x_ref, w_ref, g_ref, b_ref, of_ref, ob_ref):
    k = pl.program_id(1)
    last = DOWN_STEPS - 1

    @pl.when(k == 0)
    def _():
        of_ref[...] = jnp.dot(a_ref[...], w_ref[...], preferred_element_type=jnp.float32)

    @pl.when((k > 0) & (k < last))
    def _():
        of_ref[...] += jnp.dot(a_ref[...], w_ref[...], preferred_element_type=jnp.float32)

    @pl.when(k == last)
    def _():
        kk = D_FF - last * DOWN_K
        w = w_ref[0:kk, :]
        half = a_ref.shape[0] // 2
        outs = []
        for r in range(2):
            rows = slice(r * half, (r + 1) * half)
            y = of_ref[rows, :] + jnp.dot(a_ref[rows, 0:kk], w, preferred_element_type=jnp.float32)
            outs.append(_residual_layer_norm(x_ref[rows, :], y, g_ref[...], b_ref[...]))
        out = jnp.concatenate(outs, axis=0)
        of_ref[...] = out
        ob_ref[...] = out.astype(jnp.bfloat16)


def _ffn_down_ln(act, x, w, g, b):
    seq = x.shape[0]
    row = lambda i, k: (i, 0)
    fixed = lambda i, k: (0, 0)
    return pl.pallas_call(
        _ffn_down_ln_kernel,
        grid=(seq // DOWN_ROWS, DOWN_STEPS),
        in_specs=[pl.BlockSpec((DOWN_ROWS, DOWN_K), lambda i, k: (i, k)),
                  pl.BlockSpec((DOWN_ROWS, D_MODEL), row),
                  pl.BlockSpec((DOWN_K, D_MODEL), lambda i, k: (k, 0)),
                  pl.BlockSpec((1, D_MODEL), fixed),
                  pl.BlockSpec((1, D_MODEL), fixed)],
        out_specs=[pl.BlockSpec((DOWN_ROWS, D_MODEL), row),
                   pl.BlockSpec((DOWN_ROWS, D_MODEL), row)],
        out_shape=[jax.ShapeDtypeStruct((seq, D_MODEL), jnp.float32),
                   jax.ShapeDtypeStruct((seq, D_MODEL), jnp.bfloat16)],
        compiler_params=_params("arbitrary", "arbitrary"),
        name="ffn_down_ln",
    )(act, x, w, g, b)


def _pad_ff(a, axis):
    pad = [(0, 0)] * a.ndim
    pad[axis] = (0, FF_PAD - D_FF)
    return jnp.pad(a, pad)


def kernel(x, pool_w, pool_scale, attn_w_qkv, attn_w_o, ffn_w_up, ffn_conv_w, ffn_conv_b,
           ffn_w_down, ln_mix_g, ln_mix_b, ln_ffn_g, ln_ffn_b):
    batch, seq, _ = x.shape
    bf16 = jnp.bfloat16
    vec = lambda a: a.reshape(1, -1)
    ntri = -jnp.tril(jnp.ones((ATTN_BLOCK, ATTN_BLOCK), jnp.float32), k=-1)

    outs = []
    for bi in range(batch):
        xf = x[bi]
        xb = None
        for i in range(DEPTH):
            j = i // 2
            g, b = vec(ln_mix_g[i]), vec(ln_mix_b[i])
            if i % 2 == 0:
                xf, xb = _pool_ln(xf, pool_w, j, vec(pool_scale[j]), g, b)
            else:
                qkv = _qkv(xb, attn_w_qkv, j)
                o = _attention(qkv, ntri)
                xf, xb = _proj_ln(o, xf, attn_w_o, j, g, b)
            cw, cb = ffn_conv_w[i], ffn_conv_b[i]
            act, w_down = _ffn_up(xb, ffn_w_up, ffn_w_down, i,
                          _pad_ff(cw[:, :D_FF], 1), _pad_ff(cw[:, D_FF:], 1),
                          _pad_ff(vec(cb[:D_FF]), 1), _pad_ff(vec(cb[D_FF:]), 1))
            xf, xb = _ffn_down_ln(act, xf, w_down, vec(ln_ffn_g[i]), vec(ln_ffn_b[i]))
        outs.append(xf)
    return jnp.stack(outs, axis=0)
```

```python
import functools
import math

import jax
import jax.numpy as jnp
from jax import lax
from jax.experimental import pallas as pl
from jax.experimental.pallas import tpu as pltpu

D_MODEL = 2048
DEPTH = 4
POOL_WINDOWS = (2, 4, 8, 16)
N_POOL_GROUPS = len(POOL_WINDOWS)
POOL_GROUP_DIM = D_MODEL // N_POOL_GROUPS
HEAD_DIM = 128
N_HEADS = D_MODEL // HEAD_DIM
D_FF = 5504
CONV_WIDTH = 3
LN_EPS = 1e-5
DEEPNORM_ALPHA = (2.0 * DEPTH) ** 0.25

V7X_MXU_DIM = 256
V7X_BF16_SUBLANES = 16
V7X_LANES = 128
V7X_VMEM_LIMIT_BYTES = 60 * 1024 * 1024

HALO_ROWS = V7X_BF16_SUBLANES
POOL_ROWS = 512
QKV_ROWS, QKV_COLS = 1024, 1024
ATTN_BLOCK = V7X_MXU_DIM
ATTN_HEADS = 4
ATTN_UNROLL = 8
PROJ_ROWS = 512
UP_ROWS, UP_COLS = 1024, 512
UP_TILES = -(-D_FF // UP_COLS)
FF_PAD = UP_TILES * UP_COLS
UP_VAL_BLOCKS = UP_COLS // V7X_LANES
UP_VAL_ROW_SPLITS = (512,)
DOWN_ROWS, DOWN_K = 1024, 1024
DOWN_STEPS = -(-D_FF // DOWN_K)

LOG2_E = math.log2(math.e)


def _params(*semantics):
    return pltpu.CompilerParams(dimension_semantics=semantics,
                                vmem_limit_bytes=V7X_VMEM_LIMIT_BYTES)


def _residual_layer_norm(x, y, g, b):
    v = DEEPNORM_ALPHA * x + y
    mu = jnp.mean(v, axis=-1, keepdims=True)
    c = v - mu
    var = jnp.mean(c * c, axis=-1, keepdims=True)
    return c * lax.rsqrt(var + LN_EPS) * g + b


def _prev_halo_index(rows):
    per_tile = rows // HALO_ROWS
    return lambda i, *_: (jnp.maximum(i * per_tile - 1, 0), 0)


def _pool_ln_kernel(x_ref, halo_ref, w_ref, scale_ref, g_ref, b_ref, of_ref, ob_ref):
    i = pl.program_id(0)
    rows = x_ref.shape[0]
    x = x_ref[...]
    halo = jnp.where(i == 0, 0.0, halo_ref[...])
    t1 = i * rows + lax.broadcasted_iota(jnp.int32, (rows, 1), 0) + 1
    ys = []
    for grp, win in enumerate(POOL_WINDOWS):
        cols = slice(grp * POOL_GROUP_DIM, (grp + 1) * POOL_GROUP_DIM)
        xg = x[:, cols]
        s = jnp.concatenate([halo[:, cols], xg], axis=0)
        span = 1
        while span < win:
            s = s + pltpu.roll(s, span, axis=0)
            span *= 2
        cnt = jnp.minimum(t1, win).astype(jnp.float32)
        pooled = s[HALO_ROWS:, :] / cnt - xg
        ys.append(jnp.dot(pooled.astype(jnp.bfloat16), w_ref[grp].astype(jnp.bfloat16),
                          preferred_element_type=jnp.float32))
    y = jnp.concatenate(ys, axis=1) * scale_ref[...]
    out = _residual_layer_norm(x, y, g_ref[...], b_ref[...])
    of_ref[...] = out
    ob_ref[...] = out.astype(jnp.bfloat16)


def _pool_ln(x, w, layer, scale, g, b):
    seq = x.shape[0]
    row = lambda i: (i, 0)
    fixed = lambda i: (0, 0)
    return pl.pallas_call(
        _pool_ln_kernel,
        grid=(seq // POOL_ROWS,),
        in_specs=[pl.BlockSpec((POOL_ROWS, D_MODEL), row),
                  pl.BlockSpec((HALO_ROWS, D_MODEL), _prev_halo_index(POOL_ROWS)),
                  pl.BlockSpec((None,) + w.shape[1:], lambda i: (layer, 0, 0, 0)),
                  pl.BlockSpec((1, D_MODEL), fixed),
                  pl.BlockSpec((1, D_MODEL), fixed),
                  pl.BlockSpec((1, D_MODEL), fixed)],
        out_specs=[pl.BlockSpec((POOL_ROWS, D_MODEL), row),
                   pl.BlockSpec((POOL_ROWS, D_MODEL), row)],
        out_shape=[jax.ShapeDtypeStruct((seq, D_MODEL), jnp.float32),
                   jax.ShapeDtypeStruct((seq, D_MODEL), jnp.bfloat16)],
        compiler_params=_params("arbitrary"),
        name="pool_ln",
    )(x, x, w, scale, g, b)


def _qkv_kernel(x_ref, w_ref, o_ref, *, q_col_tiles, q_scale):
    j = pl.program_id(0)
    h = jnp.dot(x_ref[...], w_ref[...].astype(jnp.bfloat16), preferred_element_type=jnp.float32)
    h = h * jnp.where(j < q_col_tiles, q_scale, 1.0)
    o_ref[...] = h.astype(jnp.bfloat16)


def _qkv(xb, w, layer):
    seq = xb.shape[0]
    n_out = w.shape[2]
    kern = functools.partial(_qkv_kernel, q_col_tiles=D_MODEL // QKV_COLS,
                             q_scale=HEAD_DIM ** -0.5 * LOG2_E)
    return pl.pallas_call(
        kern,
        grid=(n_out // QKV_COLS, seq // QKV_ROWS),
        in_specs=[pl.BlockSpec((QKV_ROWS, D_MODEL), lambda j, i: (i, 0)),
                  pl.BlockSpec((None, D_MODEL, QKV_COLS), lambda j, i: (layer, 0, j))],
        out_specs=pl.BlockSpec((QKV_ROWS, QKV_COLS), lambda j, i: (i, j)),
        out_shape=jax.ShapeDtypeStruct((seq, n_out), jnp.bfloat16),
        compiler_params=_params("arbitrary", "arbitrary"),
        name="qkv",
    )(xb, w)


def _neg_abs(z):
    bits = lax.bitcast_convert_type(z, jnp.uint32) | jnp.uint32(0x80000000)
    return lax.bitcast_convert_type(bits, jnp.float32)


def _attn_kernel(q_ref, k_ref, v_ref, ntri_ref, o_ref, z_ref, a_ref, acc_ref, carry_ref):
    i = pl.program_id(1)
    blk = q_ref.shape[0]
    n_heads = q_ref.shape[1] // HEAD_DIM
    ntri = ntri_ref[...]
    head_lanes = [slice(hd * HEAD_DIM, (hd + 1) * HEAD_DIM) for hd in range(n_heads)]

    def logits(hd, kb):
        start = pl.multiple_of(kb * blk, blk)
        return lax.dot_general(q_ref[:, head_lanes[hd]], k_ref[pl.ds(start, blk), head_lanes[hd]],
                               (((1,), (1,)), ((), ())),
                               preferred_element_type=jnp.float32)

    def weigh(hd, kb):
        start = pl.multiple_of(kb * blk, blk)
        acc_ref[hd] += jnp.dot(a_ref[hd], v_ref[pl.ds(start, blk), head_lanes[hd]],
                               preferred_element_type=jnp.float32)

    def shape_all(slot):
        us, pbs = [], []
        for hd in range(n_heads):
            z = z_ref[slot, hd]
            p = jnp.maximum(z, 0.0) + jnp.log2(1.0 + jnp.exp2(_neg_abs(z)))
            us.append(z - p)
            pbs.append(p)
        cs_all = jnp.dot(jnp.concatenate(pbs, axis=0), ntri,
                         preferred_element_type=jnp.float32)
        for hd in range(n_heads):
            cs = cs_all[hd * blk:(hd + 1) * blk]
            carry = carry_ref[hd]
            a_ref[hd] = jnp.exp2(us[hd] + cs + carry).astype(jnp.bfloat16)
            carry_ref[hd] = carry + cs[:, 0:1] - pbs[hd][:, 0:1]

    r = lax.broadcasted_iota(jnp.int32, (blk, blk), 0)
    c = lax.broadcasted_iota(jnp.int32, (blk, blk), 1)
    for hd in range(n_heads):
        z_ref[0, hd] = jnp.where(c < r, logits(hd, i), -jnp.inf)
        a_ref[hd] = jnp.zeros((blk, blk), jnp.bfloat16)
        acc_ref[hd] = jnp.zeros((blk, HEAD_DIM), jnp.float32)
        carry_ref[hd] = jnp.zeros((blk, 1), jnp.float32)

    def trip(t, slot):
        for hd in range(n_heads):
            weigh(hd, jnp.minimum(i - t + 1, i))
        shape_all(slot)
        for hd in range(n_heads):
            z_ref[1 - slot, hd] = logits(hd, jnp.maximum(i - t - 1, 0))

    def trips(first, count):
        for n in range(count):
            trip(first + n, n % 2)

    n_trips = i + 1
    lax.fori_loop(0, n_trips // ATTN_UNROLL,
                  lambda n, _: trips(ATTN_UNROLL * n, ATTN_UNROLL) or 0, 0)
    tail = n_trips % ATTN_UNROLL
    done = n_trips - tail
    for pairs in range(1, ATTN_UNROLL // 2):
        @pl.when(tail // 2 == pairs)
        def _():
            trips(done, 2 * pairs)

    @pl.when(tail % 2 == 1)
    def _():
        trip(i, 0)

    for hd in range(n_heads):
        weigh(hd, 0)
        o_ref[:, head_lanes[hd]] = acc_ref[hd].astype(jnp.bfloat16)


def _attention(qkv, ntri):
    seq = qkv.shape[0]
    width = ATTN_HEADS * HEAD_DIM
    groups = N_HEADS // ATTN_HEADS
    return pl.pallas_call(
        _attn_kernel,
        grid=(groups, seq // ATTN_BLOCK),
        in_specs=[pl.BlockSpec((ATTN_BLOCK, width), lambda h, i: (i, h)),
                  pl.BlockSpec((seq, width), lambda h, i: (0, groups + h)),
                  pl.BlockSpec((seq, width), lambda h, i: (0, 2 * groups + h)),
                  pl.BlockSpec((ATTN_BLOCK, ATTN_BLOCK), lambda h, i: (0, 0))],
        out_specs=pl.BlockSpec((ATTN_BLOCK, width), lambda h, i: (i, h)),
        out_shape=jax.ShapeDtypeStruct((seq, D_MODEL), jnp.bfloat16),
        scratch_shapes=[pltpu.VMEM((2, ATTN_HEADS, ATTN_BLOCK, ATTN_BLOCK), jnp.float32),
                        pltpu.VMEM((ATTN_HEADS, ATTN_BLOCK, ATTN_BLOCK), jnp.bfloat16),
                        pltpu.VMEM((ATTN_HEADS, ATTN_BLOCK, HEAD_DIM), jnp.float32),
                        pltpu.VMEM((ATTN_HEADS, ATTN_BLOCK, 1), jnp.float32)],
        compiler_params=_params("arbitrary", "arbitrary"),
        name="stickbreak_attn",
    )(qkv, qkv, qkv, ntri)


def _proj_ln_kernel(o_ref, x_ref, w_ref, g_ref, b_ref, of_ref, ob_ref):
    w = w_ref[...].astype(jnp.bfloat16)
    half = o_ref.shape[0] // 2
    outs = []
    for r in range(2):
        rows = slice(r * half, (r + 1) * half)
        y = jnp.dot(o_ref[rows, :], w, preferred_element_type=jnp.float32)
        outs.append(_residual_layer_norm(x_ref[rows, :], y, g_ref[...], b_ref[...]))
    out = jnp.concatenate(outs, axis=0)
    of_ref[...] = out
    ob_ref[...] = out.astype(jnp.bfloat16)


def _proj_ln(o, x, w, layer, g, b):
    seq = x.shape[0]
    row = lambda i: (i, 0)
    fixed = lambda i: (0, 0)
    return pl.pallas_call(
        _proj_ln_kernel,
        grid=(seq // PROJ_ROWS,),
        in_specs=[pl.BlockSpec((PROJ_ROWS, D_MODEL), row),
                  pl.BlockSpec((PROJ_ROWS, D_MODEL), row),
                  pl.BlockSpec((None, D_MODEL, D_MODEL), lambda i: (layer, 0, 0),
                               pipeline_mode=pl.Buffered(1)),
                  pl.BlockSpec((1, D_MODEL), fixed),
                  pl.BlockSpec((1, D_MODEL), fixed)],
        out_specs=[pl.BlockSpec((PROJ_ROWS, D_MODEL), row),
                   pl.BlockSpec((PROJ_ROWS, D_MODEL), row)],
        out_shape=[jax.ShapeDtypeStruct((seq, D_MODEL), jnp.float32),
                   jax.ShapeDtypeStruct((seq, D_MODEL), jnp.bfloat16)],
        compiler_params=_params("arbitrary"),
        name="proj_ln",
    )(o, x, w, g, b)


def _ffn_up_kernel(x_ref, halo_ref, wg_ref, *rest):
    wv_refs = rest[:UP_VAL_BLOCKS]
    cwg_ref, cwv_ref, cbg_ref, cbv_ref, wd_ref, o_ref, wdb_ref, xe_ref = rest[UP_VAL_BLOCKS:]
    i = pl.program_id(0)

    @pl.when(pl.program_id(1) == 0)
    def _():
        xe_ref[0:HALO_ROWS, :] = jnp.where(i == 0, jnp.zeros_like(halo_ref), halo_ref[...])
        xe_ref[HALO_ROWS:, :] = x_ref[...]

    xe = xe_ref[...]

    def causal_conv(he, cw, cb):
        h1 = pltpu.roll(he, 1, axis=0)[HALO_ROWS:]
        h2 = pltpu.roll(he, 2, axis=0)[HALO_ROWS:]
        return cb + cw[0:1] * h2 + cw[1:2] * h1 + cw[2:3] * he[HALO_ROWS:]

    wg = wg_ref[...].astype(jnp.bfloat16)
    wv = jnp.concatenate([r[...].astype(jnp.bfloat16) for r in wv_refs], axis=1)
    gate = causal_conv(jnp.dot(xe, wg, preferred_element_type=jnp.float32), cwg_ref[...], cbg_ref[...])
    gate = gate * jax.nn.sigmoid(gate)
    acts = []
    for start, stop in zip((0,) + UP_VAL_ROW_SPLITS, UP_VAL_ROW_SPLITS + (o_ref.shape[0],)):
        rows = slice(start, stop)
        hv = jnp.dot(xe_ref[start:stop + HALO_ROWS, :], wv, preferred_element_type=jnp.float32)
        val = causal_conv(hv, cwv_ref[...], cbv_ref[...])
        acts.append((gate[rows] * val).astype(jnp.bfloat16))
    o_ref[...] = jnp.concatenate(acts, axis=0)
    wdb_ref[...] = wd_ref[...].astype(jnp.bfloat16)


def _ffn_up(xb, w_up, w_down, layer, cwg, cwv, cbg, cbv):
    seq = xb.shape[0]
    col = lambda i, j: (0, j)
    wd_cols = D_MODEL // (seq // UP_ROWS)
    val0 = D_FF // V7X_LANES
    last = 2 * D_FF // V7X_LANES - 1

    def val_index(m, i, j):
        return layer, 0, jnp.minimum(val0 + UP_VAL_BLOCKS * j + m, last)

    val_specs = [pl.BlockSpec((None, D_MODEL, V7X_LANES), functools.partial(val_index, m))
                 for m in range(UP_VAL_BLOCKS)]
    return pl.pallas_call(
        _ffn_up_kernel,
        grid=(seq // UP_ROWS, UP_TILES),
        in_specs=[pl.BlockSpec((UP_ROWS, D_MODEL), lambda i, j: (i, 0)),
                  pl.BlockSpec((HALO_ROWS, D_MODEL), _prev_halo_index(UP_ROWS)),
                  pl.BlockSpec((None, D_MODEL, UP_COLS), lambda i, j: (layer, 0, j)),
                  *val_specs,
                  pl.BlockSpec((CONV_WIDTH, UP_COLS), col),
                  pl.BlockSpec((CONV_WIDTH, UP_COLS), col),
                  pl.BlockSpec((1, UP_COLS), col),
                  pl.BlockSpec((1, UP_COLS), col),
                  pl.BlockSpec((None, UP_COLS, wd_cols), lambda i, j: (layer, j, i))],
        out_specs=[pl.BlockSpec((UP_ROWS, UP_COLS), lambda i, j: (i, j)),
                   pl.BlockSpec((UP_COLS, wd_cols), lambda i, j: (j, i))],
        out_shape=[jax.ShapeDtypeStruct((seq, FF_PAD), jnp.bfloat16),
                   jax.ShapeDtypeStruct((D_FF, D_MODEL), jnp.bfloat16)],
        scratch_shapes=[pltpu.VMEM((HALO_ROWS + UP_ROWS, D_MODEL), jnp.bfloat16)],
        compiler_params=_params("arbitrary", "arbitrary"),
        name="ffn_up",
    )(xb, xb, w_up, *([w_up] * UP_VAL_BLOCKS), cwg, cwv, cbg, cbv, w_down)


def _ffn_down_ln_kernel(a_ref, x_ref, w_ref, g_ref, b_ref, of_ref, ob_ref):
    k = pl.program_id(1)
    last = DOWN_STEPS - 1

    @pl.when(k == 0)
    def _():
        of_ref[...] = jnp.dot(a_ref[...], w_ref[...], preferred_element_type=jnp.float32)

    @pl.when((k > 0) & (k < last))
    def _():
        of_ref[...] += jnp.dot(a_ref[...], w_ref[...], preferred_element_type=jnp.float32)

    @pl.when(k == last)
    def _():
        kk = D_FF - last * DOWN_K
        w = w_ref[0:kk, :]
        half = a_ref.shape[0] // 2
        outs = []
        for r in range(2):
            rows = slice(r * half, (r + 1) * half)
            y = of_ref[rows, :] + jnp.dot(a_ref[rows, 0:kk], w, preferred_element_type=jnp.float32)
            outs.append(_residual_layer_norm(x_ref[rows, :], y, g_ref[...], b_ref[...]))
        out = jnp.concatenate(outs, axis=0)
        of_ref[...] = out
        ob_ref[...] = out.astype(jnp.bfloat16)


def _ffn_down_ln(act, x, w, g, b):
    seq = x.shape[0]
    row = lambda i, k: (i, 0)
    fixed = lambda i, k: (0, 0)
    return pl.pallas_call(
        _ffn_down_ln_kernel,
        grid=(seq // DOWN_ROWS, DOWN_STEPS),
        in_specs=[pl.BlockSpec((DOWN_ROWS, DOWN_K), lambda i, k: (i, k)),
                  pl.BlockSpec((DOWN_ROWS, D_MODEL), row),
                  pl.BlockSpec((DOWN_K, D_MODEL), lambda i, k: (k, 0)),
                  pl.BlockSpec((1, D_MODEL), fixed),
                  pl.BlockSpec((1, D_MODEL), fixed)],
        out_specs=[pl.BlockSpec((DOWN_ROWS, D_MODEL), row),
                   pl.BlockSpec((DOWN_ROWS, D_MODEL), row)],
        out_shape=[jax.ShapeDtypeStruct((seq, D_MODEL), jnp.float32),
                   jax.ShapeDtypeStruct((seq, D_MODEL), jnp.bfloat16)],
        compiler_params=_params("arbitrary", "arbitrary"),
        name="ffn_down_ln",
    )(act, x, w, g, b)


def _pad_ff(a, axis):
    pad = [(0, 0)] * a.ndim
    pad[axis] = (0, FF_PAD - D_FF)
    return jnp.pad(a, pad)


def kernel(x, pool_w, pool_scale, attn_w_qkv, attn_w_o, ffn_w_up, ffn_conv_w, ffn_conv_b,
           ffn_w_down, ln_mix_g, ln_mix_b, ln_ffn_g, ln_ffn_b):
    batch, seq, _ = x.shape
    bf16 = jnp.bfloat16
    vec = lambda a: a.reshape(1, -1)
    ntri = -jnp.tril(jnp.ones((ATTN_BLOCK, ATTN_BLOCK), jnp.float32), k=-1)

    outs = []
    for bi in range(batch):
        xf = x[bi]
        xb = None
        for i in range(DEPTH):
            j = i // 2
            g, b = vec(ln_mix_g[i]), vec(ln_mix_b[i])
            if i % 2 == 0:
                xf, xb = _pool_ln(xf, pool_w, j, vec(pool_scale[j]), g, b)
            else:
                qkv = _qkv(xb, attn_w_qkv, j)
                o = _attention(qkv, ntri)
                xf, xb = _proj_ln(o, xf, attn_w_o, j, g, b)
            cw, cb = ffn_conv_w[i], ffn_conv_b[i]
            act, w_down = _ffn_up(xb, ffn_w_up, ffn_w_down, i,
                          _pad_ff(cw[:, :D_FF], 1), _pad_ff(cw[:, D_FF:], 1),
                          _pad_ff(vec(cb[:D_FF]), 1), _pad_ff(vec(cb[D_FF:]), 1))
            xf, xb = _ffn_down_ln(act, xf, w_down, vec(ln_ffn_g[i]), vec(ln_ffn_b[i]))
        outs.append(xf)
    return jnp.stack(outs, axis=0)
```

```python
import functools
import math

import jax
import jax.numpy as jnp
from jax import lax
from jax.experimental import pallas as pl
from jax.experimental.pallas import tpu as pltpu

D_MODEL = 2048
DEPTH = 4
POOL_WINDOWS = (2, 4, 8, 16)
N_POOL_GROUPS = len(POOL_WINDOWS)
POOL_GROUP_DIM = D_MODEL // N_POOL_GROUPS
HEAD_DIM = 128
N_HEADS = D_MODEL // HEAD_DIM
D_FF = 5504
CONV_WIDTH = 3
LN_EPS = 1e-5
DEEPNORM_ALPHA = (2.0 * DEPTH) ** 0.25

V7X_MXU_DIM = 256
V7X_BF16_SUBLANES = 16
V7X_LANES = 128
V7X_VMEM_LIMIT_BYTES = 60 * 1024 * 1024

HALO_ROWS = V7X_BF16_SUBLANES
POOL_ROWS = 512
QKV_ROWS, QKV_COLS = 1024, 1024
ATTN_BLOCK = V7X_MXU_DIM
ATTN_HEADS = 4
ATTN_UNROLL = 8
PROJ_ROWS = 512
UP_ROWS, UP_COLS = 1024, 512
UP_TILES = -(-D_FF // UP_COLS)
FF_PAD = UP_TILES * UP_COLS
UP_VAL_BLOCKS = UP_COLS // V7X_LANES
UP_VAL_ROW_SPLITS = (512,)
DOWN_ROWS, DOWN_K = 1024, 1024
DOWN_STEPS = -(-D_FF // DOWN_K)

LOG2_E = math.log2(math.e)


def _params(*semantics):
    return pltpu.CompilerParams(dimension_semantics=semantics,
                                vmem_limit_bytes=V7X_VMEM_LIMIT_BYTES)


def _residual_layer_norm(x, y, g, b):
    v = DEEPNORM_ALPHA * x + y
    mu = jnp.mean(v, axis=-1, keepdims=True)
    c = v - mu
    var = jnp.mean(c * c, axis=-1, keepdims=True)
    return c * lax.rsqrt(var + LN_EPS) * g + b


def _prev_halo_index(rows):
    per_tile = rows // HALO_ROWS
    return lambda i, *_: (jnp.maximum(i * per_tile - 1, 0), 0)


def _pool_ln_kernel(x_ref, halo_ref, w_ref, scale_ref, g_ref, b_ref, of_ref, ob_ref):
    i = pl.program_id(0)
    rows = x_ref.shape[0]
    x = x_ref[...]
    halo = jnp.where(i == 0, 0.0, halo_ref[...])
    t1 = i * rows + lax.broadcasted_iota(jnp.int32, (rows, 1), 0) + 1
    ys = []
    for grp, win in enumerate(POOL_WINDOWS):
        cols = slice(grp * POOL_GROUP_DIM, (grp + 1) * POOL_GROUP_DIM)
        xg = x[:, cols]
        s = jnp.concatenate([halo[:, cols], xg], axis=0)
        span = 1
        while span < win:
            s = s + pltpu.roll(s, span, axis=0)
            span *= 2
        cnt = jnp.minimum(t1, win).astype(jnp.float32)
        pooled = s[HALO_ROWS:, :] / cnt - xg
        ys.append(jnp.dot(pooled.astype(jnp.bfloat16), w_ref[grp].astype(jnp.bfloat16),
                          preferred_element_type=jnp.float32))
    y = jnp.concatenate(ys, axis=1) * scale_ref[...]
    out = _residual_layer_norm(x, y, g_ref[...], b_ref[...])
    of_ref[...] = out
    ob_ref[...] = out.astype(jnp.bfloat16)


def _pool_ln(x, w, layer, scale, g, b):
    seq = x.shape[0]
    row = lambda i: (i, 0)
    fixed = lambda i: (0, 0)
    return pl.pallas_call(
        _pool_ln_kernel,
        grid=(seq // POOL_ROWS,),
        in_specs=[pl.BlockSpec((POOL_ROWS, D_MODEL), row),
                  pl.BlockSpec((HALO_ROWS, D_MODEL), _prev_halo_index(POOL_ROWS)),
                  pl.BlockSpec((None,) + w.shape[1:], lambda i: (layer, 0, 0, 0)),
                  pl.BlockSpec((1, D_MODEL), fixed),
                  pl.BlockSpec((1, D_MODEL), fixed),
                  pl.BlockSpec((1, D_MODEL), fixed)],
        out_specs=[pl.BlockSpec((POOL_ROWS, D_MODEL), row),
                   pl.BlockSpec((POOL_ROWS, D_MODEL), row)],
        out_shape=[jax.ShapeDtypeStruct((seq, D_MODEL), jnp.float32),
                   jax.ShapeDtypeStruct((seq, D_MODEL), jnp.bfloat16)],
        compiler_params=_params("arbitrary"),
        name="pool_ln",
    )(x, x, w, scale, g, b)


def _qkv_kernel(x_ref, w_ref, o_ref, *, q_col_tiles, q_scale):
    j = pl.program_id(0)
    h = jnp.dot(x_ref[...], w_ref[...].astype(jnp.bfloat16), preferred_element_type=jnp.float32)
    h = h * jnp.where(j < q_col_tiles, q_scale, 1.0)
    o_ref[...] = h.astype(jnp.bfloat16)


def _qkv(xb, w, layer):
    seq = xb.shape[0]
    n_out = w.shape[2]
    kern = functools.partial(_qkv_kernel, q_col_tiles=D_MODEL // QKV_COLS,
                             q_scale=HEAD_DIM ** -0.5 * LOG2_E)
    return pl.pallas_call(
        kern,
        grid=(n_out // QKV_COLS, seq // QKV_ROWS),
        in_specs=[pl.BlockSpec((QKV_ROWS, D_MODEL), lambda j, i: (i, 0)),
                  pl.BlockSpec((None, D_MODEL, QKV_COLS), lambda j, i: (layer, 0, j))],
        out_specs=pl.BlockSpec((QKV_ROWS, QKV_COLS), lambda j, i: (i, j)),
        out_shape=jax.ShapeDtypeStruct((seq, n_out), jnp.bfloat16),
        compiler_params=_params("arbitrary", "arbitrary"),
        name="qkv",
    )(xb, w)


def _neg_abs(z):
    bits = lax.bitcast_convert_type(z, jnp.uint32) | jnp.uint32(0x80000000)
    return lax.bitcast_convert_type(bits, jnp.float32)


def _attn_kernel(q_ref, k_ref, v_ref, tri_ref, o_ref, z_ref, a_ref, acc_ref, carry_ref, vt_ref):
    i = pl.program_id(1)
    blk = q_ref.shape[0]
    n_heads = q_ref.shape[1] // HEAD_DIM
    tri = tri_ref[...]
    head_lanes = [slice(hd * HEAD_DIM, (hd + 1) * HEAD_DIM) for hd in range(n_heads)]

    @pl.when(i == 0)
    def _():
        def fill(c, _):
            start = pl.multiple_of(c * blk, blk)
            for hd in range(n_heads):
                vt_ref[c, head_lanes[hd], :] = v_ref[pl.ds(start, blk), head_lanes[hd]].T
            return 0
        lax.fori_loop(0, k_ref.shape[0] // blk, fill, 0)

    def logits(hd, kb):
        start = pl.multiple_of(kb * blk, blk)
        return lax.dot_general(k_ref[pl.ds(start, blk), head_lanes[hd]], q_ref[:, head_lanes[hd]],
                               (((1,), (1,)), ((), ())),
                               preferred_element_type=jnp.float32)

    def weigh(hd, kb):
        acc_ref[hd] += jnp.dot(vt_ref[kb, head_lanes[hd], :], a_ref[hd],
                               preferred_element_type=jnp.float32)

    def shape_all(slot):
        for hd in range(n_heads):
            z = z_ref[slot, hd]
            p = jnp.maximum(z, 0.0) + jnp.log2(1.0 + jnp.exp2(_neg_abs(z)))
            cs = jnp.dot(tri, p, preferred_element_type=jnp.float32)
            carry = carry_ref[hd]
            a_ref[hd] = jnp.exp2(z - p + cs + carry).astype(jnp.bfloat16)
            carry_ref[hd] = carry + cs[0:1, :] - p[0:1, :]

    key = lax.broadcasted_iota(jnp.int32, (blk, blk), 0)
    qry = lax.broadcasted_iota(jnp.int32, (blk, blk), 1)
    for hd in range(n_heads):
        z_ref[0, hd] = jnp.where(key < qry, logits(hd, i), -jnp.inf)
        a_ref[hd] = jnp.zeros((blk, blk), jnp.bfloat16)
        acc_ref[hd] = jnp.zeros((HEAD_DIM, blk), jnp.float32)
        carry_ref[hd] = jnp.zeros((1, blk), jnp.float32)

    def trip(t, slot):
        for hd in range(n_heads):
            weigh(hd, jnp.minimum(i - t + 1, i))
        shape_all(slot)
        for hd in range(n_heads):
            z_ref[1 - slot, hd] = logits(hd, jnp.maximum(i - t - 1, 0))

    def trips(first, count):
        for n in range(count):
            trip(first + n, n % 2)

    n_trips = i + 1
    lax.fori_loop(0, n_trips // ATTN_UNROLL,
                  lambda n, _: trips(ATTN_UNROLL * n, ATTN_UNROLL) or 0, 0)
    tail = n_trips % ATTN_UNROLL
    done = n_trips - tail
    for pairs in range(1, ATTN_UNROLL // 2):
        @pl.when(tail // 2 == pairs)
        def _():
            trips(done, 2 * pairs)

    @pl.when(tail % 2 == 1)
    def _():
        trip(i, 0)

    for hd in range(n_heads):
        weigh(hd, 0)
        o_ref[:, head_lanes[hd]] = acc_ref[hd].T.astype(jnp.bfloat16)


def _attention(qkv, tri):
    seq = qkv.shape[0]
    width = ATTN_HEADS * HEAD_DIM
    groups = N_HEADS // ATTN_HEADS
    return pl.pallas_call(
        _attn_kernel,
        grid=(groups, seq // ATTN_BLOCK),
        in_specs=[pl.BlockSpec((ATTN_BLOCK, width), lambda h, i: (i, h)),
                  pl.BlockSpec((seq, width), lambda h, i: (0, groups + h)),
                  pl.BlockSpec((seq, width), lambda h, i: (0, 2 * groups + h)),
                  pl.BlockSpec((ATTN_BLOCK, ATTN_BLOCK), lambda h, i: (0, 0))],
        out_specs=pl.BlockSpec((ATTN_BLOCK, width), lambda h, i: (i, h)),
        out_shape=jax.ShapeDtypeStruct((seq, D_MODEL), jnp.bfloat16),
        scratch_shapes=[pltpu.VMEM((2, ATTN_HEADS, ATTN_BLOCK, ATTN_BLOCK), jnp.float32),
                        pltpu.VMEM((ATTN_HEADS, ATTN_BLOCK, ATTN_BLOCK), jnp.bfloat16),
                        pltpu.VMEM((ATTN_HEADS, HEAD_DIM, ATTN_BLOCK), jnp.float32),
                        pltpu.VMEM((ATTN_HEADS, 1, ATTN_BLOCK), jnp.float32),
                        pltpu.VMEM((seq // ATTN_BLOCK, width, ATTN_BLOCK), jnp.bfloat16)],
        compiler_params=_params("arbitrary", "arbitrary"),
        name="stickbreak_attn",
    )(qkv, qkv, qkv, tri)


def _proj_ln_kernel(o_ref, x_ref, w_ref, g_ref, b_ref, of_ref, ob_ref):
    w = w_ref[...].astype(jnp.bfloat16)
    half = o_ref.shape[0] // 2
    outs = []
    for r in range(2):
        rows = slice(r * half, (r + 1) * half)
        y = jnp.dot(o_ref[rows, :], w, preferred_element_type=jnp.float32)
        outs.append(_residual_layer_norm(x_ref[rows, :], y, g_ref[...], b_ref[...]))
    out = jnp.concatenate(outs, axis=0)
    of_ref[...] = out
    ob_ref[...] = out.astype(jnp.bfloat16)


def _proj_ln(o, x, w, layer, g, b):
    seq = x.shape[0]
    row = lambda i: (i, 0)
    fixed = lambda i: (0, 0)
    return pl.pallas_call(
        _proj_ln_kernel,
        grid=(seq // PROJ_ROWS,),
        in_specs=[pl.BlockSpec((PROJ_ROWS, D_MODEL), row),
                  pl.BlockSpec((PROJ_ROWS, D_MODEL), row),
                  pl.BlockSpec((None, D_MODEL, D_MODEL), lambda i: (layer, 0, 0),
                               pipeline_mode=pl.Buffered(1)),
                  pl.BlockSpec((1, D_MODEL), fixed),
                  pl.BlockSpec((1, D_MODEL), fixed)],
        out_specs=[pl.BlockSpec((PROJ_ROWS, D_MODEL), row),
                   pl.BlockSpec((PROJ_ROWS, D_MODEL), row)],
        out_shape=[jax.ShapeDtypeStruct((seq, D_MODEL), jnp.float32),
                   jax.ShapeDtypeStruct((seq, D_MODEL), jnp.bfloat16)],
        compiler_params=_params("arbitrary"),
        name="proj_ln",
    )(o, x, w, g, b)


def _ffn_up_kernel(x_ref, halo_ref, wg_ref, *rest):
    wv_refs = rest[:UP_VAL_BLOCKS]
    cwg_ref, cwv_ref, cbg_ref, cbv_ref, wd_ref, o_ref, wdb_ref, xe_ref = rest[UP_VAL_BLOCKS:]
    i = pl.program_id(0)

    @pl.when(pl.program_id(1) == 0)
    def _():
        xe_ref[0:HALO_ROWS, :] = jnp.where(i == 0, jnp.zeros_like(halo_ref), halo_ref[...])
        xe_ref[HALO_ROWS:, :] = x_ref[...]

    xe = xe_ref[...]

    def causal_conv(he, cw, cb):
        h1 = pltpu.roll(he, 1, axis=0)[HALO_ROWS:]
        h2 = pltpu.roll(he, 2, axis=0)[HALO_ROWS:]
        return cb + cw[0:1] * h2 + cw[1:2] * h1 + cw[2:3] * he[HALO_ROWS:]

    wg = wg_ref[...].astype(jnp.bfloat16)
    wv = jnp.concatenate([r[...].astype(jnp.bfloat16) for r in wv_refs], axis=1)
    gate = causal_conv(jnp.dot(xe, wg, preferred_element_type=jnp.float32), cwg_ref[...], cbg_ref[...])
    gate = gate * jax.nn.sigmoid(gate)
    acts = []
    for start, stop in zip((0,) + UP_VAL_ROW_SPLITS, UP_VAL_ROW_SPLITS + (o_ref.shape[0],)):
        rows = slice(start, stop)
        hv = jnp.dot(xe_ref[start:stop + HALO_ROWS, :], wv, preferred_element_type=jnp.float32)
        val = causal_conv(hv, cwv_ref[...], cbv_ref[...])
        acts.append((gate[rows] * val).astype(jnp.bfloat16))
    o_ref[...] = jnp.concatenate(acts, axis=0)
    wdb_ref[...] = wd_ref[...].astype(jnp.bfloat16)


def _ffn_up(xb, w_up, w_down, layer, cwg, cwv, cbg, cbv):
    seq = xb.shape[0]
    col = lambda i, j: (0, j)
    wd_cols = D_MODEL // (seq // UP_ROWS)
    val0 = D_FF // V7X_LANES
    last = 2 * D_FF // V7X_LANES - 1

    def val_index(m, i, j):
        return layer, 0, jnp.minimum(val0 + UP_VAL_BLOCKS * j + m, last)

    val_specs = [pl.BlockSpec((None, D_MODEL, V7X_LANES), functools.partial(val_index, m))
                 for m in range(UP_VAL_BLOCKS)]
    return pl.pallas_call(
        _ffn_up_kernel,
        grid=(seq // UP_ROWS, UP_TILES),
        in_specs=[pl.BlockSpec((UP_ROWS, D_MODEL), lambda i, j: (i, 0)),
                  pl.BlockSpec((HALO_ROWS, D_MODEL), _prev_halo_index(UP_ROWS)),
                  pl.BlockSpec((None, D_MODEL, UP_COLS), lambda i, j: (layer, 0, j)),
                  *val_specs,
                  pl.BlockSpec((CONV_WIDTH, UP_COLS), col),
                  pl.BlockSpec((CONV_WIDTH, UP_COLS), col),
                  pl.BlockSpec((1, UP_COLS), col),
                  pl.BlockSpec((1, UP_COLS), col),
                  pl.BlockSpec((None, UP_COLS, wd_cols), lambda i, j: (layer, j, i))],
        out_specs=[pl.BlockSpec((UP_ROWS, UP_COLS), lambda i, j: (i, j)),
                   pl.BlockSpec((UP_COLS, wd_cols), lambda i, j: (j, i))],
        out_shape=[jax.ShapeDtypeStruct((seq, FF_PAD), jnp.bfloat16),
                   jax.ShapeDtypeStruct((D_FF, D_MODEL), jnp.bfloat16)],
        scratch_shapes=[pltpu.VMEM((HALO_ROWS + UP_ROWS, D_MODEL), jnp.bfloat16)],
        compiler_params=_params("arbitrary", "arbitrary"),
        name="ffn_up",
    )(xb, xb, w_up, *([w_up] * UP_VAL_BLOCKS), cwg, cwv, cbg, cbv, w_down)


def _ffn_down_ln_kernel(a_ref, x_ref, w_ref, g_ref, b_ref, of_ref, ob_ref):
    k = pl.program_id(1)
    last = DOWN_STEPS - 1

    @pl.when(k == 0)
    def _():
        of_ref[...] = jnp.dot(a_ref[...], w_ref[...], preferred_element_type=jnp.float32)

    @pl.when((k > 0) & (k < last))
    def _():
        of_ref[...] += jnp.dot(a_ref[...], w_ref[...], preferred_element_type=jnp.float32)

    @pl.when(k == last)
    def _():
        kk = D_FF - last * DOWN_K
        w = w_ref[0:kk, :]
        half = a_ref.shape[0] // 2
        outs = []
        for r in range(2):
            rows = slice(r * half, (r + 1) * half)
            y = of_ref[rows, :] + jnp.dot(a_ref[rows, 0:kk], w, preferred_element_type=jnp.float32)
            outs.append(_residual_layer_norm(x_ref[rows, :], y, g_ref[...], b_ref[...]))
        out = jnp.concatenate(outs, axis=0)
        of_ref[...] = out
        ob_ref[...] = out.astype(jnp.bfloat16)


def _ffn_down_ln(act, x, w, g, b):
    seq = x.shape[0]
    row = lambda i, k: (i, 0)
    fixed = lambda i, k: (0, 0)
    return pl.pallas_call(
        _ffn_down_ln_kernel,
        grid=(seq // DOWN_ROWS, DOWN_STEPS),
        in_specs=[pl.BlockSpec((DOWN_ROWS, DOWN_K), lambda i, k: (i, k)),
                  pl.BlockSpec((DOWN_ROWS, D_MODEL), row),
                  pl.BlockSpec((DOWN_K, D_MODEL), lambda i, k: (k, 0)),
                  pl.BlockSpec((1, D_MODEL), fixed),
                  pl.BlockSpec((1, D_MODEL), fixed)],
        out_specs=[pl.BlockSpec((DOWN_ROWS, D_MODEL), row),
                   pl.BlockSpec((DOWN_ROWS, D_MODEL), row)],
        out_shape=[jax.ShapeDtypeStruct((seq, D_MODEL), jnp.float32),
                   jax.ShapeDtypeStruct((seq, D_MODEL), jnp.bfloat16)],
        compiler_params=_params("arbitrary", "arbitrary"),
        name="ffn_down_ln",
    )(act, x, w, g, b)


def _pad_ff(a, axis):
    pad = [(0, 0)] * a.ndim
    pad[axis] = (0, FF_PAD - D_FF)
    return jnp.pad(a, pad)


def kernel(x, pool_w, pool_scale, attn_w_qkv, attn_w_o, ffn_w_up, ffn_conv_w, ffn_conv_b,
           ffn_w_down, ln_mix_g, ln_mix_b, ln_ffn_g, ln_ffn_b):
    batch, seq, _ = x.shape
    bf16 = jnp.bfloat16
    vec = lambda a: a.reshape(1, -1)
    tri = -jnp.triu(jnp.ones((ATTN_BLOCK, ATTN_BLOCK), jnp.float32), k=1)

    outs = []
    for bi in range(batch):
        xf = x[bi]
        xb = None
        for i in range(DEPTH):
            j = i // 2
            g, b = vec(ln_mix_g[i]), vec(ln_mix_b[i])
            if i % 2 == 0:
                xf, xb = _pool_ln(xf, pool_w, j, vec(pool_scale[j]), g, b)
            else:
                qkv = _qkv(xb, attn_w_qkv, j)
                o = _attention(qkv, tri)
                xf, xb = _proj_ln(o, xf, attn_w_o, j, g, b)
            cw, cb = ffn_conv_w[i], ffn_conv_b[i]
            act, w_down = _ffn_up(xb, ffn_w_up, ffn_w_down, i,
                          _pad_ff(cw[:, :D_FF], 1), _pad_ff(cw[:, D_FF:], 1),
                          _pad_ff(vec(cb[:D_FF]), 1), _pad_ff(vec(cb[D_FF:]), 1))
            xf, xb = _ffn_down_ln(act, xf, w_down, vec(ln_ffn_g[i]), vec(ln_ffn_b[i]))
        outs.append(xf)
    return jnp.stack(outs, axis=0)
```

```python
import functools
import math

import jax
import jax.numpy as jnp
from jax import lax
from jax.experimental import pallas as pl
from jax.experimental.pallas import tpu as pltpu

D_MODEL = 2048
DEPTH = 4
POOL_WINDOWS = (2, 4, 8, 16)
N_POOL_GROUPS = len(POOL_WINDOWS)
POOL_GROUP_DIM = D_MODEL // N_POOL_GROUPS
HEAD_DIM = 128
N_HEADS = D_MODEL // HEAD_DIM
D_FF = 5504
CONV_WIDTH = 3
LN_EPS = 1e-5
DEEPNORM_ALPHA = (2.0 * DEPTH) ** 0.25

V7X_MXU_DIM = 256
V7X_BF16_SUBLANES = 16
V7X_LANES = 128
V7X_VMEM_LIMIT_BYTES = 60 * 1024 * 1024

HALO_ROWS = V7X_BF16_SUBLANES
POOL_ROWS = 512
QKV_ROWS, QKV_COLS = 1024, 1024
ATTN_BLOCK = V7X_MXU_DIM
ATTN_HEADS = 4
ATTN_UNROLL = 8
PROJ_ROWS = 512
UP_ROWS, UP_COLS = 1024, 512
UP_TILES = -(-D_FF // UP_COLS)
FF_PAD = UP_TILES * UP_COLS
UP_VAL_BLOCKS = UP_COLS // V7X_LANES
UP_VAL_ROW_SPLITS = (512,)
DOWN_ROWS, DOWN_K = 1024, 1024
DOWN_STEPS = -(-D_FF // DOWN_K)

LOG2_E = math.log2(math.e)


def _params(*semantics):
    return pltpu.CompilerParams(dimension_semantics=semantics,
                                vmem_limit_bytes=V7X_VMEM_LIMIT_BYTES)


def _residual_layer_norm(x, y, g, b):
    v = DEEPNORM_ALPHA * x + y
    mu = jnp.mean(v, axis=-1, keepdims=True)
    c = v - mu
    var = jnp.mean(c * c, axis=-1, keepdims=True)
    return c * lax.rsqrt(var + LN_EPS) * g + b


def _prev_halo_index(rows):
    per_tile = rows // HALO_ROWS
    return lambda i, *_: (jnp.maximum(i * per_tile - 1, 0), 0)


def _pool_ln_kernel(x_ref, halo_ref, w_ref, scale_ref, g_ref, b_ref, of_ref, ob_ref):
    i = pl.program_id(0)
    rows = x_ref.shape[0]
    x = x_ref[...]
    halo = jnp.where(i == 0, 0.0, halo_ref[...])
    t1 = i * rows + lax.broadcasted_iota(jnp.int32, (rows, 1), 0) + 1
    ys = []
    for grp, win in enumerate(POOL_WINDOWS):
        cols = slice(grp * POOL_GROUP_DIM, (grp + 1) * POOL_GROUP_DIM)
        xg = x[:, cols]
        s = jnp.concatenate([halo[:, cols], xg], axis=0)
        span = 1
        while span < win:
            s = s + pltpu.roll(s, span, axis=0)
            span *= 2
        cnt = jnp.minimum(t1, win).astype(jnp.float32)
        pooled = s[HALO_ROWS:, :] / cnt - xg
        ys.append(jnp.dot(pooled.astype(jnp.bfloat16), w_ref[grp].astype(jnp.bfloat16),
                          preferred_element_type=jnp.float32))
    y = jnp.concatenate(ys, axis=1) * scale_ref[...]
    out = _residual_layer_norm(x, y, g_ref[...], b_ref[...])
    of_ref[...] = out
    ob_ref[...] = out.astype(jnp.bfloat16)


def _pool_ln(x, w, layer, scale, g, b):
    seq = x.shape[0]
    row = lambda i: (i, 0)
    fixed = lambda i: (0, 0)
    return pl.pallas_call(
        _pool_ln_kernel,
        grid=(seq // POOL_ROWS,),
        in_specs=[pl.BlockSpec((POOL_ROWS, D_MODEL), row),
                  pl.BlockSpec((HALO_ROWS, D_MODEL), _prev_halo_index(POOL_ROWS)),
                  pl.BlockSpec((None,) + w.shape[1:], lambda i: (layer, 0, 0, 0)),
                  pl.BlockSpec((1, D_MODEL), fixed),
                  pl.BlockSpec((1, D_MODEL), fixed),
                  pl.BlockSpec((1, D_MODEL), fixed)],
        out_specs=[pl.BlockSpec((POOL_ROWS, D_MODEL), row),
                   pl.BlockSpec((POOL_ROWS, D_MODEL), row)],
        out_shape=[jax.ShapeDtypeStruct((seq, D_MODEL), jnp.float32),
                   jax.ShapeDtypeStruct((seq, D_MODEL), jnp.bfloat16)],
        compiler_params=_params("arbitrary"),
        name="pool_ln",
    )(x, x, w, scale, g, b)


def _qkv_kernel(x_ref, w_ref, o_ref, *, q_col_tiles, q_scale):
    j = pl.program_id(0)
    h = jnp.dot(x_ref[...], w_ref[...].astype(jnp.bfloat16), preferred_element_type=jnp.float32)
    h = h * jnp.where(j < q_col_tiles, q_scale, 1.0)
    o_ref[...] = h.astype(jnp.bfloat16)


def _qkv(xb, w, layer):
    seq = xb.shape[0]
    n_out = w.shape[2]
    kern = functools.partial(_qkv_kernel, q_col_tiles=D_MODEL // QKV_COLS,
                             q_scale=HEAD_DIM ** -0.5 * LOG2_E)
    return pl.pallas_call(
        kern,
        grid=(n_out // QKV_COLS, seq // QKV_ROWS),
        in_specs=[pl.BlockSpec((QKV_ROWS, D_MODEL), lambda j, i: (i, 0)),
                  pl.BlockSpec((None, D_MODEL, QKV_COLS), lambda j, i: (layer, 0, j))],
        out_specs=pl.BlockSpec((QKV_ROWS, QKV_COLS), lambda j, i: (i, j)),
        out_shape=jax.ShapeDtypeStruct((seq, n_out), jnp.bfloat16),
        compiler_params=_params("arbitrary", "arbitrary"),
        name="qkv",
    )(xb, w)


def _neg_abs(z):
    bits = lax.bitcast_convert_type(z, jnp.uint32) | jnp.uint32(0x80000000)
    return lax.bitcast_convert_type(bits, jnp.float32)


def _attn_kernel(q_ref, k_ref, v_ref, ntri_ref, o_ref, z_ref, a_ref, acc_ref, carry_ref, kt_ref):
    i = pl.program_id(1)
    blk = q_ref.shape[0]
    n_heads = q_ref.shape[1] // HEAD_DIM
    ntri = ntri_ref[...]
    head_lanes = [slice(hd * HEAD_DIM, (hd + 1) * HEAD_DIM) for hd in range(n_heads)]

    @pl.when(i == 0)
    def _():
        def fill(c, _):
            start = pl.multiple_of(c * blk, blk)
            for hd in range(n_heads):
                kt_ref[c, head_lanes[hd], :] = k_ref[pl.ds(start, blk), head_lanes[hd]].T
            return 0
        lax.fori_loop(0, k_ref.shape[0] // blk, fill, 0)

    def logits(hd, kb):
        return jnp.dot(q_ref[:, head_lanes[hd]], kt_ref[kb, head_lanes[hd], :],
                       preferred_element_type=jnp.float32)

    def weigh(hd, kb):
        start = pl.multiple_of(kb * blk, blk)
        acc_ref[hd] += jnp.dot(a_ref[hd], v_ref[pl.ds(start, blk), head_lanes[hd]],
                               preferred_element_type=jnp.float32)

    def shape_all(slot):
        us, pbs = [], []
        for hd in range(n_heads):
            z = z_ref[slot, hd]
            p = jnp.maximum(z, 0.0) + jnp.log2(1.0 + jnp.exp2(_neg_abs(z)))
            us.append(z - p)
            pbs.append(p)
        cs_all = jnp.dot(jnp.concatenate(pbs, axis=0), ntri,
                         preferred_element_type=jnp.float32)
        for hd in range(n_heads):
            cs = cs_all[hd * blk:(hd + 1) * blk]
            carry = carry_ref[hd]
            a_ref[hd] = jnp.exp2(us[hd] + cs + carry).astype(jnp.bfloat16)
            carry_ref[hd] = carry + cs[:, 0:1] - pbs[hd][:, 0:1]

    r = lax.broadcasted_iota(jnp.int32, (blk, blk), 0)
    c = lax.broadcasted_iota(jnp.int32, (blk, blk), 1)
    for hd in range(n_heads):
        z_ref[0, hd] = jnp.where(c < r, logits(hd, i), -jnp.inf)
        a_ref[hd] = jnp.zeros((blk, blk), jnp.bfloat16)
        acc_ref[hd] = jnp.zeros((blk, HEAD_DIM), jnp.float32)
        carry_ref[hd] = jnp.zeros((blk, 1), jnp.float32)

    def trip(t, slot):
        for hd in range(n_heads):
            weigh(hd, jnp.minimum(i - t + 1, i))
        shape_all(slot)
        for hd in range(n_heads):
            z_ref[1 - slot, hd] = logits(hd, jnp.maximum(i - t - 1, 0))

    def trips(first, count):
        for n in range(count):
            trip(first + n, n % 2)

    n_trips = i + 1
    lax.fori_loop(0, n_trips // ATTN_UNROLL,
                  lambda n, _: trips(ATTN_UNROLL * n, ATTN_UNROLL) or 0, 0)
    tail = n_trips % ATTN_UNROLL
    done = n_trips - tail
    for pairs in range(1, ATTN_UNROLL // 2):
        @pl.when(tail // 2 == pairs)
        def _():
            trips(done, 2 * pairs)

    @pl.when(tail % 2 == 1)
    def _():
        trip(i, 0)

    for hd in range(n_heads):
        weigh(hd, 0)
        o_ref[:, head_lanes[hd]] = acc_ref[hd].astype(jnp.bfloat16)


def _attention(qkv, ntri):
    seq = qkv.shape[0]
    width = ATTN_HEADS * HEAD_DIM
    groups = N_HEADS // ATTN_HEADS
    return pl.pallas_call(
        _attn_kernel,
        grid=(groups, seq // ATTN_BLOCK),
        in_specs=[pl.BlockSpec((ATTN_BLOCK, width), lambda h, i: (i, h)),
                  pl.BlockSpec((seq, width), lambda h, i: (0, groups + h)),
                  pl.BlockSpec((seq, width), lambda h, i: (0, 2 * groups + h)),
                  pl.BlockSpec((ATTN_BLOCK, ATTN_BLOCK), lambda h, i: (0, 0))],
        out_specs=pl.BlockSpec((ATTN_BLOCK, width), lambda h, i: (i, h)),
        out_shape=jax.ShapeDtypeStruct((seq, D_MODEL), jnp.bfloat16),
        scratch_shapes=[pltpu.VMEM((2, ATTN_HEADS, ATTN_BLOCK, ATTN_BLOCK), jnp.float32),
                        pltpu.VMEM((ATTN_HEADS, ATTN_BLOCK, ATTN_BLOCK), jnp.bfloat16),
                        pltpu.VMEM((ATTN_HEADS, ATTN_BLOCK, HEAD_DIM), jnp.float32),
                        pltpu.VMEM((ATTN_HEADS, ATTN_BLOCK, 1), jnp.float32),
                        pltpu.VMEM((seq // ATTN_BLOCK, width, ATTN_BLOCK), jnp.bfloat16)],
        compiler_params=_params("arbitrary", "arbitrary"),
        name="stickbreak_attn",
    )(qkv, qkv, qkv, ntri)


def _proj_ln_kernel(o_ref, x_ref, w_ref, g_ref, b_ref, of_ref, ob_ref):
    w = w_ref[...].astype(jnp.bfloat16)
    half = o_ref.shape[0] // 2
    outs = []
    for r in range(2):
        rows = slice(r * half, (r + 1) * half)
        y = jnp.dot(o_ref[rows, :], w, preferred_element_type=jnp.float32)
        outs.append(_residual_layer_norm(x_ref[rows, :], y, g_ref[...], b_ref[...]))
    out = jnp.concatenate(outs, axis=0)
    of_ref[...] = out
    ob_ref[...] = out.astype(jnp.bfloat16)


def _proj_ln(o, x, w, layer, g, b):
    seq = x.shape[0]
    row = lambda i: (i, 0)
    fixed = lambda i: (0, 0)
    return pl.pallas_call(
        _proj_ln_kernel,
        grid=(seq // PROJ_ROWS,),
        in_specs=[pl.BlockSpec((PROJ_ROWS, D_MODEL), row),
                  pl.BlockSpec((PROJ_ROWS, D_MODEL), row),
                  pl.BlockSpec((None, D_MODEL, D_MODEL), lambda i: (layer, 0, 0),
                               pipeline_mode=pl.Buffered(1)),
                  pl.BlockSpec((1, D_MODEL), fixed),
                  pl.BlockSpec((1, D_MODEL), fixed)],
        out_specs=[pl.BlockSpec((PROJ_ROWS, D_MODEL), row),
                   pl.BlockSpec((PROJ_ROWS, D_MODEL), row)],
        out_shape=[jax.ShapeDtypeStruct((seq, D_MODEL), jnp.float32),
                   jax.ShapeDtypeStruct((seq, D_MODEL), jnp.bfloat16)],
        compiler_params=_params("arbitrary"),
        name="proj_ln",
    )(o, x, w, g, b)


def _ffn_up_kernel(x_ref, halo_ref, wg_ref, *rest):
    wv_refs = rest[:UP_VAL_BLOCKS]
    cwg_ref, cwv_ref, cbg_ref, cbv_ref, wd_ref, o_ref, wdb_ref, xe_ref = rest[UP_VAL_BLOCKS:]
    i = pl.program_id(0)

    @pl.when(pl.program_id(1) == 0)
    def _():
        xe_ref[0:HALO_ROWS, :] = jnp.where(i == 0, jnp.zeros_like(halo_ref), halo_ref[...])
        xe_ref[HALO_ROWS:, :] = x_ref[...]

    xe = xe_ref[...]

    def causal_conv(he, cw, cb):
        h1 = pltpu.roll(he, 1, axis=0)[HALO_ROWS:]
        h2 = pltpu.roll(he, 2, axis=0)[HALO_ROWS:]
        return cb + cw[0:1] * h2 + cw[1:2] * h1 + cw[2:3] * he[HALO_ROWS:]

    wg = wg_ref[...].astype(jnp.bfloat16)
    wv = jnp.concatenate([r[...].astype(jnp.bfloat16) for r in wv_refs], axis=1)
    gate = causal_conv(jnp.dot(xe, wg, preferred_element_type=jnp.float32), cwg_ref[...], cbg_ref[...])
    gate = gate * jax.nn.sigmoid(gate)
    acts = []
    for start, stop in zip((0,) + UP_VAL_ROW_SPLITS, UP_VAL_ROW_SPLITS + (o_ref.shape[0],)):
        rows = slice(start, stop)
        hv = jnp.dot(xe_ref[start:stop + HALO_ROWS, :], wv, preferred_element_type=jnp.float32)
        val = causal_conv(hv, cwv_ref[...], cbv_ref[...])
        acts.append((gate[rows] * val).astype(jnp.bfloat16))
    o_ref[...] = jnp.concatenate(acts, axis=0)
    wdb_ref[...] = wd_ref[...].astype(jnp.bfloat16)


def _ffn_up(xb, w_up, w_down, layer, cwg, cwv, cbg, cbv):
    seq = xb.shape[0]
    col = lambda i, j: (0, j)
    wd_cols = D_MODEL // (seq // UP_ROWS)
    val0 = D_FF // V7X_LANES
    last = 2 * D_FF // V7X_LANES - 1

    def val_index(m, i, j):
        return layer, 0, jnp.minimum(val0 + UP_VAL_BLOCKS * j + m, last)

    val_specs = [pl.BlockSpec((None, D_MODEL, V7X_LANES), functools.partial(val_index, m))
                 for m in range(UP_VAL_BLOCKS)]
    return pl.pallas_call(
        _ffn_up_kernel,
        grid=(seq // UP_ROWS, UP_TILES),
        in_specs=[pl.BlockSpec((UP_ROWS, D_MODEL), lambda i, j: (i, 0)),
                  pl.BlockSpec((HALO_ROWS, D_MODEL), _prev_halo_index(UP_ROWS)),
                  pl.BlockSpec((None, D_MODEL, UP_COLS), lambda i, j: (layer, 0, j)),
                  *val_specs,
                  pl.BlockSpec((CONV_WIDTH, UP_COLS), col),
                  pl.BlockSpec((CONV_WIDTH, UP_COLS), col),
                  pl.BlockSpec((1, UP_COLS), col),
                  pl.BlockSpec((1, UP_COLS), col),
                  pl.BlockSpec((None, UP_COLS, wd_cols), lambda i, j: (layer, j, i))],
        out_specs=[pl.BlockSpec((UP_ROWS, UP_COLS), lambda i, j: (i, j)),
                   pl.BlockSpec((UP_COLS, wd_cols), lambda i, j: (j, i))],
        out_shape=[jax.ShapeDtypeStruct((seq, FF_PAD), jnp.bfloat16),
                   jax.ShapeDtypeStruct((D_FF, D_MODEL), jnp.bfloat16)],
        scratch_shapes=[pltpu.VMEM((HALO_ROWS + UP_ROWS, D_MODEL), jnp.bfloat16)],
        compiler_params=_params("arbitrary", "arbitrary"),
        name="ffn_up",
    )(xb, xb, w_up, *([w_up] * UP_VAL_BLOCKS), cwg, cwv, cbg, cbv, w_down)


def _ffn_down_ln_kernel(a_ref, x_ref, w_ref, g_ref, b_ref, of_ref, ob_ref):
    k = pl.program_id(1)
    last = DOWN_STEPS - 1

    @pl.when(k == 0)
    def _():
        of_ref[...] = jnp.dot(a_ref[...], w_ref[...], preferred_element_type=jnp.float32)

    @pl.when((k > 0) & (k < last))
    def _():
        of_ref[...] += jnp.dot(a_ref[...], w_ref[...], preferred_element_type=jnp.float32)

    @pl.when(k == last)
    def _():
        kk = D_FF - last * DOWN_K
        w = w_ref[0:kk, :]
        half = a_ref.shape[0] // 2
        outs = []
        for r in range(2):
            rows = slice(r * half, (r + 1) * half)
            y = of_ref[rows, :] + jnp.dot(a_ref[rows, 0:kk], w, preferred_element_type=jnp.float32)
            outs.append(_residual_layer_norm(x_ref[rows, :], y, g_ref[...], b_ref[...]))
        out = jnp.concatenate(outs, axis=0)
        of_ref[...] = out
        ob_ref[...] = out.astype(jnp.bfloat16)


def _ffn_down_ln(act, x, w, g, b):
    seq = x.shape[0]
    row = lambda i, k: (i, 0)
    fixed = lambda i, k: (0, 0)
    return pl.pallas_call(
        _ffn_down_ln_kernel,
        grid=(seq // DOWN_ROWS, DOWN_STEPS),
        in_specs=[pl.BlockSpec((DOWN_ROWS, DOWN_K), lambda i, k: (i, k)),
                  pl.BlockSpec((DOWN_ROWS, D_MODEL), row),
                  pl.BlockSpec((DOWN_K, D_MODEL), lambda i, k: (k, 0)),
                  pl.BlockSpec((1, D_MODEL), fixed),
                  pl.BlockSpec((1, D_MODEL), fixed)],
        out_specs=[pl.BlockSpec((DOWN_ROWS, D_MODEL), row),
                   pl.BlockSpec((DOWN_ROWS, D_MODEL), row)],
        out_shape=[jax.ShapeDtypeStruct((seq, D_MODEL), jnp.float32),
                   jax.ShapeDtypeStruct((seq, D_MODEL), jnp.bfloat16)],
        compiler_params=_params("arbitrary", "arbitrary"),
        name="ffn_down_ln",
    )(act, x, w, g, b)


def _pad_ff(a, axis):
    pad = [(0, 0)] * a.ndim
    pad[axis] = (0, FF_PAD - D_FF)
    return jnp.pad(a, pad)


def kernel(x, pool_w, pool_scale, attn_w_qkv, attn_w_o, ffn_w_up, ffn_conv_w, ffn_conv_b,
           ffn_w_down, ln_mix_g, ln_mix_b, ln_ffn_g, ln_ffn_b):
    batch, seq, _ = x.shape
    bf16 = jnp.bfloat16
    vec = lambda a: a.reshape(1, -1)
    ntri = -jnp.tril(jnp.ones((ATTN_BLOCK, ATTN_BLOCK), jnp.float32), k=-1)

    outs = []
    for bi in range(batch):
        xf = x[bi]
        xb = None
        for i in range(DEPTH):
            j = i // 2
            g, b = vec(ln_mix_g[i]), vec(ln_mix_b[i])
            if i % 2 == 0:
                xf, xb = _pool_ln(xf, pool_w, j, vec(pool_scale[j]), g, b)
            else:
                qkv = _qkv(xb, attn_w_qkv, j)
                o = _attention(qkv, ntri)
                xf, xb = _proj_ln(o, xf, attn_w_o, j, g, b)
            cw, cb = ffn_conv_w[i], ffn_conv_b[i]
            act, w_down = _ffn_up(xb, ffn_w_up, ffn_w_down, i,
                          _pad_ff(cw[:, :D_FF], 1), _pad_ff(cw[:, D_FF:], 1),
                          _pad_ff(vec(cb[:D_FF]), 1), _pad_ff(vec(cb[D_FF:]), 1))
            xf, xb = _ffn_down_ln(act, xf, w_down, vec(ln_ffn_g[i]), vec(ln_ffn_b[i]))
        outs.append(xf)
    return jnp.stack(outs, axis=0)
```

```python
import functools
import math

import jax
import jax.numpy as jnp
from jax import lax
from jax.experimental import pallas as pl
from jax.experimental.pallas import tpu as pltpu

D_MODEL = 2048
DEPTH = 4
POOL_WINDOWS = (2, 4, 8, 16)
N_POOL_GROUPS = len(POOL_WINDOWS)
POOL_GROUP_DIM = D_MODEL // N_POOL_GROUPS
HEAD_DIM = 128
N_HEADS = D_MODEL // HEAD_DIM
D_FF = 5504
CONV_WIDTH = 3
LN_EPS = 1e-5
DEEPNORM_ALPHA = (2.0 * DEPTH) ** 0.25

V7X_MXU_DIM = 256
V7X_BF16_SUBLANES = 16
V7X_LANES = 128
V7X_VMEM_LIMIT_BYTES = 60 * 1024 * 1024

HALO_ROWS = V7X_BF16_SUBLANES
POOL_ROWS = 512
QKV_ROWS, QKV_COLS = 1024, 1024
ATTN_BLOCK = V7X_MXU_DIM
ATTN_HEADS = 4
ATTN_UNROLL = 8
ATTN_Q_BLOCKS = 2
PROJ_ROWS = 512
UP_ROWS, UP_COLS = 1024, 512
UP_TILES = -(-D_FF // UP_COLS)
FF_PAD = UP_TILES * UP_COLS
UP_VAL_BLOCKS = UP_COLS // V7X_LANES
UP_VAL_ROW_SPLITS = (512,)
DOWN_ROWS, DOWN_K = 1024, 1024
DOWN_STEPS = -(-D_FF // DOWN_K)

LOG2_E = math.log2(math.e)


def _params(*semantics):
    return pltpu.CompilerParams(dimension_semantics=semantics,
                                vmem_limit_bytes=V7X_VMEM_LIMIT_BYTES)


def _residual_layer_norm(x, y, g, b):
    v = DEEPNORM_ALPHA * x + y
    mu = jnp.mean(v, axis=-1, keepdims=True)
    c = v - mu
    var = jnp.mean(c * c, axis=-1, keepdims=True)
    return c * lax.rsqrt(var + LN_EPS) * g + b


def _prev_halo_index(rows):
    per_tile = rows // HALO_ROWS
    return lambda i, *_: (jnp.maximum(i * per_tile - 1, 0), 0)


def _pool_ln_kernel(x_ref, halo_ref, w_ref, scale_ref, g_ref, b_ref, of_ref, ob_ref):
    i = pl.program_id(0)
    rows = x_ref.shape[0]
    x = x_ref[...]
    halo = jnp.where(i == 0, 0.0, halo_ref[...])
    t1 = i * rows + lax.broadcasted_iota(jnp.int32, (rows, 1), 0) + 1
    ys = []
    for grp, win in enumerate(POOL_WINDOWS):
        cols = slice(grp * POOL_GROUP_DIM, (grp + 1) * POOL_GROUP_DIM)
        xg = x[:, cols]
        s = jnp.concatenate([halo[:, cols], xg], axis=0)
        span = 1
        while span < win:
            s = s + pltpu.roll(s, span, axis=0)
            span *= 2
        cnt = jnp.minimum(t1, win).astype(jnp.float32)
        pooled = s[HALO_ROWS:, :] / cnt - xg
        ys.append(jnp.dot(pooled.astype(jnp.bfloat16), w_ref[grp].astype(jnp.bfloat16),
                          preferred_element_type=jnp.float32))
    y = jnp.concatenate(ys, axis=1) * scale_ref[...]
    out = _residual_layer_norm(x, y, g_ref[...], b_ref[...])
    of_ref[...] = out
    ob_ref[...] = out.astype(jnp.bfloat16)


def _pool_ln(x, w, layer, scale, g, b):
    seq = x.shape[0]
    row = lambda i: (i, 0)
    fixed = lambda i: (0, 0)
    return pl.pallas_call(
        _pool_ln_kernel,
        grid=(seq // POOL_ROWS,),
        in_specs=[pl.BlockSpec((POOL_ROWS, D_MODEL), row),
                  pl.BlockSpec((HALO_ROWS, D_MODEL), _prev_halo_index(POOL_ROWS)),
                  pl.BlockSpec((None,) + w.shape[1:], lambda i: (layer, 0, 0, 0)),
                  pl.BlockSpec((1, D_MODEL), fixed),
                  pl.BlockSpec((1, D_MODEL), fixed),
                  pl.BlockSpec((1, D_MODEL), fixed)],
        out_specs=[pl.BlockSpec((POOL_ROWS, D_MODEL), row),
                   pl.BlockSpec((POOL_ROWS, D_MODEL), row)],
        out_shape=[jax.ShapeDtypeStruct((seq, D_MODEL), jnp.float32),
                   jax.ShapeDtypeStruct((seq, D_MODEL), jnp.bfloat16)],
        compiler_params=_params("arbitrary"),
        name="pool_ln",
    )(x, x, w, scale, g, b)


def _qkv_kernel(x_ref, w_ref, o_ref, *, q_col_tiles, q_scale):
    j = pl.program_id(0)
    h = jnp.dot(x_ref[...], w_ref[...].astype(jnp.bfloat16), preferred_element_type=jnp.float32)
    h = h * jnp.where(j < q_col_tiles, q_scale, 1.0)
    o_ref[...] = h.astype(jnp.bfloat16)


def _qkv(xb, w, layer):
    seq = xb.shape[0]
    n_out = w.shape[2]
    kern = functools.partial(_qkv_kernel, q_col_tiles=D_MODEL // QKV_COLS,
                             q_scale=HEAD_DIM ** -0.5 * LOG2_E)
    return pl.pallas_call(
        kern,
        grid=(n_out // QKV_COLS, seq // QKV_ROWS),
        in_specs=[pl.BlockSpec((QKV_ROWS, D_MODEL), lambda j, i: (i, 0)),
                  pl.BlockSpec((None, D_MODEL, QKV_COLS), lambda j, i: (layer, 0, j))],
        out_specs=pl.BlockSpec((QKV_ROWS, QKV_COLS), lambda j, i: (i, j)),
        out_shape=jax.ShapeDtypeStruct((seq, n_out), jnp.bfloat16),
        compiler_params=_params("arbitrary", "arbitrary"),
        name="qkv",
    )(xb, w)


def _neg_abs(z):
    bits = lax.bitcast_convert_type(z, jnp.uint32) | jnp.uint32(0x80000000)
    return lax.bitcast_convert_type(bits, jnp.float32)


def _attn_block(i, rows, q_ref, k_ref, v_ref, ntri_ref, o_ref, z_ref, a_ref, acc_ref, carry_ref):
    blk = ATTN_BLOCK
    n_heads = q_ref.shape[1] // HEAD_DIM
    ntri = ntri_ref[...]
    head_lanes = [slice(hd * HEAD_DIM, (hd + 1) * HEAD_DIM) for hd in range(n_heads)]

    def logits(hd, kb):
        start = pl.multiple_of(kb * blk, blk)
        return lax.dot_general(q_ref[rows, head_lanes[hd]], k_ref[pl.ds(start, blk), head_lanes[hd]],
                               (((1,), (1,)), ((), ())),
                               preferred_element_type=jnp.float32)

    def weigh(hd, kb):
        start = pl.multiple_of(kb * blk, blk)
        acc_ref[hd] += jnp.dot(a_ref[hd], v_ref[pl.ds(start, blk), head_lanes[hd]],
                               preferred_element_type=jnp.float32)

    def shape_all(slot):
        us, pbs = [], []
        for hd in range(n_heads):
            z = z_ref[slot, hd]
            p = jnp.maximum(z, 0.0) + jnp.log2(1.0 + jnp.exp2(_neg_abs(z)))
            us.append(z - p)
            pbs.append(p)
        cs_all = jnp.dot(jnp.concatenate(pbs, axis=0), ntri,
                         preferred_element_type=jnp.float32)
        for hd in range(n_heads):
            cs = cs_all[hd * blk:(hd + 1) * blk]
            carry = carry_ref[hd]
            a_ref[hd] = jnp.exp2(us[hd] + cs + carry).astype(jnp.bfloat16)
            carry_ref[hd] = carry + cs[:, 0:1] - pbs[hd][:, 0:1]

    r = lax.broadcasted_iota(jnp.int32, (blk, blk), 0)
    c = lax.broadcasted_iota(jnp.int32, (blk, blk), 1)
    for hd in range(n_heads):
        z_ref[0, hd] = jnp.where(c < r, logits(hd, i), -jnp.inf)
        a_ref[hd] = jnp.zeros((blk, blk), jnp.bfloat16)
        acc_ref[hd] = jnp.zeros((blk, HEAD_DIM), jnp.float32)
        carry_ref[hd] = jnp.zeros((blk, 1), jnp.float32)

    def trip(t, slot):
        for hd in range(n_heads):
            weigh(hd, jnp.minimum(i - t + 1, i))
        shape_all(slot)
        for hd in range(n_heads):
            z_ref[1 - slot, hd] = logits(hd, jnp.maximum(i - t - 1, 0))

    def trips(first, count):
        for n in range(count):
            trip(first + n, n % 2)

    n_trips = i + 1
    lax.fori_loop(0, n_trips // ATTN_UNROLL,
                  lambda n, _: trips(ATTN_UNROLL * n, ATTN_UNROLL) or 0, 0)
    tail = n_trips % ATTN_UNROLL
    done = n_trips - tail
    for pairs in range(1, ATTN_UNROLL // 2):
        @pl.when(tail // 2 == pairs)
        def _():
            trips(done, 2 * pairs)

    @pl.when(tail % 2 == 1)
    def _():
        trip(i, 0)

    for hd in range(n_heads):
        weigh(hd, 0)
        o_ref[rows, head_lanes[hd]] = acc_ref[hd].astype(jnp.bfloat16)


def _attn_kernel(q_ref, k_ref, v_ref, ntri_ref, o_ref, *scratch):
    def one_block(sub, _):
        rows = pl.ds(pl.multiple_of(sub * ATTN_BLOCK, ATTN_BLOCK), ATTN_BLOCK)
        _attn_block(ATTN_Q_BLOCKS * pl.program_id(1) + sub, rows,
                    q_ref, k_ref, v_ref, ntri_ref, o_ref, *scratch)
        return 0

    lax.fori_loop(0, ATTN_Q_BLOCKS, one_block, 0)


def _attention(qkv, ntri):
    seq = qkv.shape[0]
    width = ATTN_HEADS * HEAD_DIM
    groups = N_HEADS // ATTN_HEADS
    return pl.pallas_call(
        _attn_kernel,
        grid=(groups, seq // (ATTN_Q_BLOCKS * ATTN_BLOCK)),
        in_specs=[pl.BlockSpec((ATTN_Q_BLOCKS * ATTN_BLOCK, width), lambda h, i: (i, h)),
                  pl.BlockSpec((seq, width), lambda h, i: (0, groups + h)),
                  pl.BlockSpec((seq, width), lambda h, i: (0, 2 * groups + h)),
                  pl.BlockSpec((ATTN_BLOCK, ATTN_BLOCK), lambda h, i: (0, 0))],
        out_specs=pl.BlockSpec((ATTN_Q_BLOCKS * ATTN_BLOCK, width), lambda h, i: (i, h)),
        out_shape=jax.ShapeDtypeStruct((seq, D_MODEL), jnp.bfloat16),
        scratch_shapes=[pltpu.VMEM((2, ATTN_HEADS, ATTN_BLOCK, ATTN_BLOCK), jnp.float32),
                        pltpu.VMEM((ATTN_HEADS, ATTN_BLOCK, ATTN_BLOCK), jnp.bfloat16),
                        pltpu.VMEM((ATTN_HEADS, ATTN_BLOCK, HEAD_DIM), jnp.float32),
                        pltpu.VMEM((ATTN_HEADS, ATTN_BLOCK, 1), jnp.float32)],
        compiler_params=_params("arbitrary", "arbitrary"),
        name="stickbreak_attn",
    )(qkv, qkv, qkv, ntri)


def _proj_ln_kernel(o_ref, x_ref, w_ref, g_ref, b_ref, of_ref, ob_ref):
    w = w_ref[...].astype(jnp.bfloat16)
    half = o_ref.shape[0] // 2
    outs = []
    for r in range(2):
        rows = slice(r * half, (r + 1) * half)
        y = jnp.dot(o_ref[rows, :], w, preferred_element_type=jnp.float32)
        outs.append(_residual_layer_norm(x_ref[rows, :], y, g_ref[...], b_ref[...]))
    out = jnp.concatenate(outs, axis=0)
    of_ref[...] = out
    ob_ref[...] = out.astype(jnp.bfloat16)


def _proj_ln(o, x, w, layer, g, b):
    seq = x.shape[0]
    row = lambda i: (i, 0)
    fixed = lambda i: (0, 0)
    return pl.pallas_call(
        _proj_ln_kernel,
        grid=(seq // PROJ_ROWS,),
        in_specs=[pl.BlockSpec((PROJ_ROWS, D_MODEL), row),
                  pl.BlockSpec((PROJ_ROWS, D_MODEL), row),
                  pl.BlockSpec((None, D_MODEL, D_MODEL), lambda i: (layer, 0, 0),
                               pipeline_mode=pl.Buffered(1)),
                  pl.BlockSpec((1, D_MODEL), fixed),
                  pl.BlockSpec((1, D_MODEL), fixed)],
        out_specs=[pl.BlockSpec((PROJ_ROWS, D_MODEL), row),
                   pl.BlockSpec((PROJ_ROWS, D_MODEL), row)],
        out_shape=[jax.ShapeDtypeStruct((seq, D_MODEL), jnp.float32),
                   jax.ShapeDtypeStruct((seq, D_MODEL), jnp.bfloat16)],
        compiler_params=_params("arbitrary"),
        name="proj_ln",
    )(o, x, w, g, b)


def _ffn_up_kernel(x_ref, halo_ref, wg_ref, *rest):
    wv_refs = rest[:UP_VAL_BLOCKS]
    cwg_ref, cwv_ref, cbg_ref, cbv_ref, wd_ref, o_ref, wdb_ref, xe_ref = rest[UP_VAL_BLOCKS:]
    i = pl.program_id(0)

    @pl.when(pl.program_id(1) == 0)
    def _():
        xe_ref[0:HALO_ROWS, :] = jnp.where(i == 0, jnp.zeros_like(halo_ref), halo_ref[...])
        xe_ref[HALO_ROWS:, :] = x_ref[...]

    xe = xe_ref[...]

    def causal_conv(he, cw, cb):
        h1 = pltpu.roll(he, 1, axis=0)[HALO_ROWS:]
        h2 = pltpu.roll(he, 2, axis=0)[HALO_ROWS:]
        return cb + cw[0:1] * h2 + cw[1:2] * h1 + cw[2:3] * he[HALO_ROWS:]

    wg = wg_ref[...].astype(jnp.bfloat16)
    wv = jnp.concatenate([r[...].astype(jnp.bfloat16) for r in wv_refs], axis=1)
    gate = causal_conv(jnp.dot(xe, wg, preferred_element_type=jnp.float32), cwg_ref[...], cbg_ref[...])
    gate = gate * jax.nn.sigmoid(gate)
    acts = []
    for start, stop in zip((0,) + UP_VAL_ROW_SPLITS, UP_VAL_ROW_SPLITS + (o_ref.shape[0],)):
        rows = slice(start, stop)
        hv = jnp.dot(xe_ref[start:stop + HALO_ROWS, :], wv, preferred_element_type=jnp.float32)
        val = causal_conv(hv, cwv_ref[...], cbv_ref[...])
        acts.append((gate[rows] * val).astype(jnp.bfloat16))
    o_ref[...] = jnp.concatenate(acts, axis=0)
    wdb_ref[...] = wd_ref[...].astype(jnp.bfloat16)


def _ffn_up(xb, w_up, w_down, layer, cwg, cwv, cbg, cbv):
    seq = xb.shape[0]
    col = lambda i, j: (0, j)
    wd_cols = D_MODEL // (seq // UP_ROWS)
    val0 = D_FF // V7X_LANES
    last = 2 * D_FF // V7X_LANES - 1

    def val_index(m, i, j):
        return layer, 0, jnp.minimum(val0 + UP_VAL_BLOCKS * j + m, last)

    val_specs = [pl.BlockSpec((None, D_MODEL, V7X_LANES), functools.partial(val_index, m))
                 for m in range(UP_VAL_BLOCKS)]
    return pl.pallas_call(
        _ffn_up_kernel,
        grid=(seq // UP_ROWS, UP_TILES),
        in_specs=[pl.BlockSpec((UP_ROWS, D_MODEL), lambda i, j: (i, 0)),
                  pl.BlockSpec((HALO_ROWS, D_MODEL), _prev_halo_index(UP_ROWS)),
                  pl.BlockSpec((None, D_MODEL, UP_COLS), lambda i, j: (layer, 0, j)),
                  *val_specs,
                  pl.BlockSpec((CONV_WIDTH, UP_COLS), col),
                  pl.BlockSpec((CONV_WIDTH, UP_COLS), col),
                  pl.BlockSpec((1, UP_COLS), col),
                  pl.BlockSpec((1, UP_COLS), col),
                  pl.BlockSpec((None, UP_COLS, wd_cols), lambda i, j: (layer, j, i))],
        out_specs=[pl.BlockSpec((UP_ROWS, UP_COLS), lambda i, j: (i, j)),
                   pl.BlockSpec((UP_COLS, wd_cols), lambda i, j: (j, i))],
        out_shape=[jax.ShapeDtypeStruct((seq, FF_PAD), jnp.bfloat16),
                   jax.ShapeDtypeStruct((D_FF, D_MODEL), jnp.bfloat16)],
        scratch_shapes=[pltpu.VMEM((HALO_ROWS + UP_ROWS, D_MODEL), jnp.bfloat16)],
        compiler_params=_params("arbitrary", "arbitrary"),
        name="ffn_up",
    )(xb, xb, w_up, *([w_up] * UP_VAL_BLOCKS), cwg, cwv, cbg, cbv, w_down)


def _ffn_down_ln_kernel(a_ref, x_ref, w_ref, g_ref, b_ref, of_ref, ob_ref):
    k = pl.program_id(1)
    last = DOWN_STEPS - 1

    @pl.when(k == 0)
    def _():
        of_ref[...] = jnp.dot(a_ref[...], w_ref[...], preferred_element_type=jnp.float32)

    @pl.when((k > 0) & (k < last))
    def _():
        of_ref[...] += jnp.dot(a_ref[...], w_ref[...], preferred_element_type=jnp.float32)

    @pl.when(k == last)
    def _():
        kk = D_FF - last * DOWN_K
        w = w_ref[0:kk, :]
        half = a_ref.shape[0] // 2
        outs = []
        for r in range(2):
            rows = slice(r * half, (r + 1) * half)
            y = of_ref[rows, :] + jnp.dot(a_ref[rows, 0:kk], w, preferred_element_type=jnp.float32)
            outs.append(_residual_layer_norm(x_ref[rows, :], y, g_ref[...], b_ref[...]))
        out = jnp.concatenate(outs, axis=0)
        of_ref[...] = out
        ob_ref[...] = out.astype(jnp.bfloat16)


def _ffn_down_ln(act, x, w, g, b):
    seq = x.shape[0]
    row = lambda i, k: (i, 0)
    fixed = lambda i, k: (0, 0)
    return pl.pallas_call(
        _ffn_down_ln_kernel,
        grid=(seq // DOWN_ROWS, DOWN_STEPS),
        in_specs=[pl.BlockSpec((DOWN_ROWS, DOWN_K), lambda i, k: (i, k)),
                  pl.BlockSpec((DOWN_ROWS, D_MODEL), row),
                  pl.BlockSpec((DOWN_K, D_MODEL), lambda i, k: (k, 0)),
                  pl.BlockSpec((1, D_MODEL), fixed),
                  pl.BlockSpec((1, D_MODEL), fixed)],
        out_specs=[pl.BlockSpec((DOWN_ROWS, D_MODEL), row),
                   pl.BlockSpec((DOWN_ROWS, D_MODEL), row)],
        out_shape=[jax.ShapeDtypeStruct((seq, D_MODEL), jnp.float32),
                   jax.ShapeDtypeStruct((seq, D_MODEL), jnp.bfloat16)],
        compiler_params=_params("arbitrary", "arbitrary"),
        name="ffn_down_ln",
    )(act, x, w, g, b)


def _pad_ff(a, axis):
    pad = [(0, 0)] * a.ndim
    pad[axis] = (0, FF_PAD - D_FF)
    return jnp.pad(a, pad)


def kernel(x, pool_w, pool_scale, attn_w_qkv, attn_w_o, ffn_w_up, ffn_conv_w, ffn_conv_b,
           ffn_w_down, ln_mix_g, ln_mix_b, ln_ffn_g, ln_ffn_b):
    batch, seq, _ = x.shape
    bf16 = jnp.bfloat16
    vec = lambda a: a.reshape(1, -1)
    ntri = -jnp.tril(jnp.ones((ATTN_BLOCK, ATTN_BLOCK), jnp.float32), k=-1)

    outs = []
    for bi in range(batch):
        xf = x[bi]
        xb = None
        for i in range(DEPTH):
            j = i // 2
            g, b = vec(ln_mix_g[i]), vec(ln_mix_b[i])
            if i % 2 == 0:
                xf, xb = _pool_ln(xf, pool_w, j, vec(pool_scale[j]), g, b)
            else:
                qkv = _qkv(xb, attn_w_qkv, j)
                o = _attention(qkv, ntri)
                xf, xb = _proj_ln(o, xf, attn_w_o, j, g, b)
            cw, cb = ffn_conv_w[i], ffn_conv_b[i]
            act, w_down = _ffn_up(xb, ffn_w_up, ffn_w_down, i,
                          _pad_ff(cw[:, :D_FF], 1), _pad_ff(cw[:, D_FF:], 1),
                          _pad_ff(vec(cb[:D_FF]), 1), _pad_ff(vec(cb[D_FF:]), 1))
            xf, xb = _ffn_down_ln(act, xf, w_down, vec(ln_ffn_g[i]), vec(ln_ffn_b[i]))
        outs.append(xf)
    return jnp.stack(outs, axis=0)
```

```python
import functools
import math

import jax
import jax.numpy as jnp
from jax import lax
from jax.experimental import pallas as pl
from jax.experimental.pallas import tpu as pltpu

D_MODEL = 2048
DEPTH = 4
POOL_WINDOWS = (2, 4, 8, 16)
N_POOL_GROUPS = len(POOL_WINDOWS)
POOL_GROUP_DIM = D_MODEL // N_POOL_GROUPS
HEAD_DIM = 128
N_HEADS = D_MODEL // HEAD_DIM
D_FF = 5504
CONV_WIDTH = 3
LN_EPS = 1e-5
DEEPNORM_ALPHA = (2.0 * DEPTH) ** 0.25

V7X_MXU_DIM = 256
V7X_BF16_SUBLANES = 16
V7X_LANES = 128
V7X_VMEM_LIMIT_BYTES = 60 * 1024 * 1024

HALO_ROWS = V7X_BF16_SUBLANES
POOL_ROWS = 512
QKV_ROWS, QKV_COLS = 1024, 1024
ATTN_BLOCK = V7X_MXU_DIM
ATTN_HEADS = 4
ATTN_UNROLL = 8
PROJ_ROWS = 512
UP_ROWS, UP_COLS = 1024, 512
UP_TILES = -(-D_FF // UP_COLS)
FF_PAD = UP_TILES * UP_COLS
UP_VAL_BLOCKS = UP_COLS // V7X_LANES
UP_VAL_ROW_SPLITS = (512,)
DOWN_ROWS, DOWN_K = 1024, 1024
DOWN_STEPS = -(-D_FF // DOWN_K)

LOG2_E = math.log2(math.e)


def _params(*semantics):
    return pltpu.CompilerParams(dimension_semantics=semantics,
                                vmem_limit_bytes=V7X_VMEM_LIMIT_BYTES)


def _residual_layer_norm(x, y, g, b):
    v = DEEPNORM_ALPHA * x + y
    mu = jnp.mean(v, axis=-1, keepdims=True)
    c = v - mu
    var = jnp.mean(c * c, axis=-1, keepdims=True)
    return c * lax.rsqrt(var + LN_EPS) * g + b


def _prev_halo_index(rows):
    per_tile = rows // HALO_ROWS
    return lambda i, *_: (jnp.maximum(i * per_tile - 1, 0), 0)


def _pool_ln_kernel(x_ref, halo_ref, w_ref, scale_ref, g_ref, b_ref, of_ref, ob_ref):
    i = pl.program_id(0)
    rows = x_ref.shape[0]
    x = x_ref[...]
    halo = jnp.where(i == 0, 0.0, halo_ref[...])
    t1 = i * rows + lax.broadcasted_iota(jnp.int32, (rows, 1), 0) + 1
    ys = []
    for grp, win in enumerate(POOL_WINDOWS):
        cols = slice(grp * POOL_GROUP_DIM, (grp + 1) * POOL_GROUP_DIM)
        xg = x[:, cols]
        s = jnp.concatenate([halo[:, cols], xg], axis=0)
        span = 1
        while span < win:
            s = s + pltpu.roll(s, span, axis=0)
            span *= 2
        cnt = jnp.minimum(t1, win).astype(jnp.float32)
        pooled = s[HALO_ROWS:, :] / cnt - xg
        ys.append(jnp.dot(pooled.astype(jnp.bfloat16), w_ref[grp].astype(jnp.bfloat16),
                          preferred_element_type=jnp.float32))
    y = jnp.concatenate(ys, axis=1) * scale_ref[...]
    out = _residual_layer_norm(x, y, g_ref[...], b_ref[...])
    of_ref[...] = out
    ob_ref[...] = out.astype(jnp.bfloat16)


def _pool_ln(x, w, layer, scale, g, b):
    seq = x.shape[0]
    row = lambda i: (i, 0)
    fixed = lambda i: (0, 0)
    return pl.pallas_call(
        _pool_ln_kernel,
        grid=(seq // POOL_ROWS,),
        in_specs=[pl.BlockSpec((POOL_ROWS, D_MODEL), row),
                  pl.BlockSpec((HALO_ROWS, D_MODEL), _prev_halo_index(POOL_ROWS)),
                  pl.BlockSpec((None,) + w.shape[1:], lambda i: (layer, 0, 0, 0)),
                  pl.BlockSpec((1, D_MODEL), fixed),
                  pl.BlockSpec((1, D_MODEL), fixed),
                  pl.BlockSpec((1, D_MODEL), fixed)],
        out_specs=[pl.BlockSpec((POOL_ROWS, D_MODEL), row),
                   pl.BlockSpec((POOL_ROWS, D_MODEL), row)],
        out_shape=[jax.ShapeDtypeStruct((seq, D_MODEL), jnp.float32),
                   jax.ShapeDtypeStruct((seq, D_MODEL), jnp.bfloat16)],
        compiler_params=_params("arbitrary"),
        name="pool_ln",
    )(x, x, w, scale, g, b)


def _qkv_kernel(x_ref, w_ref, o_ref, *, q_col_tiles, q_scale):
    j = pl.program_id(0)
    h = jnp.dot(x_ref[...], w_ref[...].astype(jnp.bfloat16), preferred_element_type=jnp.float32)
    h = h * jnp.where(j < q_col_tiles, q_scale, 1.0)
    o_ref[...] = h.astype(jnp.bfloat16)


def _qkv(xb, w, layer):
    seq = xb.shape[0]
    n_out = w.shape[2]
    kern = functools.partial(_qkv_kernel, q_col_tiles=D_MODEL // QKV_COLS,
                             q_scale=HEAD_DIM ** -0.5 * LOG2_E)
    return pl.pallas_call(
        kern,
        grid=(n_out // QKV_COLS, seq // QKV_ROWS),
        in_specs=[pl.BlockSpec((QKV_ROWS, D_MODEL), lambda j, i: (i, 0)),
                  pl.BlockSpec((None, D_MODEL, QKV_COLS), lambda j, i: (layer, 0, j))],
        out_specs=pl.BlockSpec((QKV_ROWS, QKV_COLS), lambda j, i: (i, j)),
        out_shape=jax.ShapeDtypeStruct((seq, n_out), jnp.bfloat16),
        compiler_params=_params("arbitrary", "arbitrary"),
        name="qkv",
    )(xb, w)


def _neg_abs(z):
    bits = lax.bitcast_convert_type(z, jnp.uint32) | jnp.uint32(0x80000000)
    return lax.bitcast_convert_type(bits, jnp.float32)


def _attn_kernel(q_ref, k_ref, v_ref, ntri_ref, o_ref, z_ref, a_ref, acc_ref, carry_ref):
    i = pl.program_id(1)
    blk = q_ref.shape[0]
    n_heads = q_ref.shape[1] // HEAD_DIM
    ntri = ntri_ref[...]
    head_lanes = [slice(hd * HEAD_DIM, (hd + 1) * HEAD_DIM) for hd in range(n_heads)]

    def logits(hd, kb):
        start = pl.multiple_of(kb * blk, blk)
        return lax.dot_general(q_ref[:, head_lanes[hd]], k_ref[pl.ds(start, blk), head_lanes[hd]],
                               (((1,), (1,)), ((), ())),
                               preferred_element_type=jnp.float32)

    def weigh(hd, kb):
        start = pl.multiple_of(kb * blk, blk)
        acc_ref[hd] += jnp.dot(a_ref[hd], v_ref[pl.ds(start, blk), head_lanes[hd]],
                               preferred_element_type=jnp.float32)

    def shape_all(slot):
        us, pbs = [], []
        for hd in range(n_heads):
            z = z_ref[slot, hd]
            p = jnp.maximum(z, 0.0) + jnp.log2(1.0 + jnp.exp2(_neg_abs(z)))
            us.append(z - p)
            pbs.append(p)
        cs_all = jnp.dot(jnp.concatenate(pbs, axis=0), ntri,
                         preferred_element_type=jnp.float32)
        for hd in range(n_heads):
            cs = cs_all[hd * blk:(hd + 1) * blk]
            carry = carry_ref[hd]
            a_ref[hd] = jnp.exp2(us[hd] + cs + carry).astype(jnp.bfloat16)
            carry_ref[hd] = carry + cs[:, 0:1] - pbs[hd][:, 0:1]

    r = lax.broadcasted_iota(jnp.int32, (blk, blk), 0)
    c = lax.broadcasted_iota(jnp.int32, (blk, blk), 1)
    for hd in range(n_heads):
        z_ref[0, hd] = jnp.where(c < r, logits(hd, i), -jnp.inf)
        a_ref[hd] = jnp.zeros((blk, blk), jnp.bfloat16)
        acc_ref[hd] = jnp.zeros((blk, HEAD_DIM), jnp.float32)
        carry_ref[hd] = jnp.zeros((blk, 1), jnp.float32)

    def trip(t, slot):
        for hd in range(n_heads):
            weigh(hd, jnp.minimum(i - t + 1, i))
        shape_all(slot)
        for hd in range(n_heads):
            z_ref[1 - slot, hd] = logits(hd, jnp.maximum(i - t - 1, 0))

    def trips(first, count):
        for n in range(count):
            trip(first + n, n % 2)

    n_trips = i + 1
    lax.fori_loop(0, n_trips // ATTN_UNROLL,
                  lambda n, _: trips(ATTN_UNROLL * n, ATTN_UNROLL) or 0, 0)
    tail = n_trips % ATTN_UNROLL
    done = n_trips - tail
    for pairs in range(1, ATTN_UNROLL // 2):
        @pl.when(tail // 2 == pairs)
        def _():
            trips(done, 2 * pairs)

    @pl.when(tail % 2 == 1)
    def _():
        trip(i, 0)

    for hd in range(n_heads):
        weigh(hd, 0)
        o_ref[:, head_lanes[hd]] = acc_ref[hd].astype(jnp.bfloat16)


def _attention(qkv, ntri):
    seq = qkv.shape[0]
    width = ATTN_HEADS * HEAD_DIM
    groups = N_HEADS // ATTN_HEADS
    return pl.pallas_call(
        _attn_kernel,
        grid=(groups, seq // ATTN_BLOCK),
        in_specs=[pl.BlockSpec((ATTN_BLOCK, width), lambda h, i: (i, h)),
                  pl.BlockSpec((seq, width), lambda h, i: (0, groups + h)),
                  pl.BlockSpec((seq, width), lambda h, i: (0, 2 * groups + h)),
                  pl.BlockSpec((ATTN_BLOCK, ATTN_BLOCK), lambda h, i: (0, 0))],
        out_specs=pl.BlockSpec((ATTN_BLOCK, width), lambda h, i: (i, h)),
        out_shape=jax.ShapeDtypeStruct((seq, D_MODEL), jnp.bfloat16),
        scratch_shapes=[pltpu.VMEM((2, ATTN_HEADS, ATTN_BLOCK, ATTN_BLOCK), jnp.float32),
                        pltpu.VMEM((ATTN_HEADS, ATTN_BLOCK, ATTN_BLOCK), jnp.bfloat16),
                        pltpu.VMEM((ATTN_HEADS, ATTN_BLOCK, HEAD_DIM), jnp.float32),
                        pltpu.VMEM((ATTN_HEADS, ATTN_BLOCK, 1), jnp.float32)],
        compiler_params=_params("arbitrary", "arbitrary"),
        name="stickbreak_attn",
    )(qkv, qkv, qkv, ntri)


def _proj_ln_kernel(o_ref, x_ref, w_ref, g_ref, b_ref, of_ref, ob_ref):
    w = w_ref[...].astype(jnp.bfloat16)
    half = o_ref.shape[0] // 2
    outs = []
    for r in range(2):
        rows = slice(r * half, (r + 1) * half)
        y = jnp.dot(o_ref[rows, :], w, preferred_element_type=jnp.float32)
        outs.append(_residual_layer_norm(x_ref[rows, :], y, g_ref[...], b_ref[...]))
    out = jnp.concatenate(outs, axis=0)
    of_ref[...] = out
    ob_ref[...] = out.astype(jnp.bfloat16)


def _proj_ln(o, x, w, layer, g, b):
    seq = x.shape[0]
    row = lambda i: (i, 0)
    fixed = lambda i: (0, 0)
    return pl.pallas_call(
        _proj_ln_kernel,
        grid=(seq // PROJ_ROWS,),
        in_specs=[pl.BlockSpec((PROJ_ROWS, D_MODEL), row),
                  pl.BlockSpec((PROJ_ROWS, D_MODEL), row),
                  pl.BlockSpec((None, D_MODEL, D_MODEL), lambda i: (layer, 0, 0),
                               pipeline_mode=pl.Buffered(1)),
                  pl.BlockSpec((1, D_MODEL), fixed),
                  pl.BlockSpec((1, D_MODEL), fixed)],
        out_specs=[pl.BlockSpec((PROJ_ROWS, D_MODEL), row),
                   pl.BlockSpec((PROJ_ROWS, D_MODEL), row)],
        out_shape=[jax.ShapeDtypeStruct((seq, D_MODEL), jnp.float32),
                   jax.ShapeDtypeStruct((seq, D_MODEL), jnp.bfloat16)],
        compiler_params=_params("arbitrary"),
        name="proj_ln",
    )(o, x, w, g, b)


def _ffn_up_kernel(x_ref, halo_ref, wg_ref, *rest):
    wv_refs = rest[:UP_VAL_BLOCKS]
    cwg_ref, cwv_ref, cbg_ref, cbv_ref, wd_ref, o_ref, wdb_ref, xe_ref = rest[UP_VAL_BLOCKS:]
    i = pl.program_id(0)

    @pl.when(pl.program_id(1) == 0)
    def _():
        xe_ref[0:HALO_ROWS, :] = jnp.where(i == 0, jnp.zeros_like(halo_ref), halo_ref[...])
        xe_ref[HALO_ROWS:, :] = x_ref[...]

    xe = xe_ref[...]

    def causal_conv(he, cw, cb):
        h1 = pltpu.roll(he, 1, axis=0)[HALO_ROWS:]
        h2 = pltpu.roll(he, 2, axis=0)[HALO_ROWS:]
        return cb + cw[0:1] * h2 + cw[1:2] * h1 + cw[2:3] * he[HALO_ROWS:]

    wg = wg_ref[...].astype(jnp.bfloat16)
    wv = jnp.concatenate([r[...].astype(jnp.bfloat16) for r in wv_refs], axis=1)
    gate = causal_conv(jnp.dot(xe, wg, preferred_element_type=jnp.float32), cwg_ref[...], cbg_ref[...])
    gate = gate * jax.nn.sigmoid(gate)
    acts = []
    for start, stop in zip((0,) + UP_VAL_ROW_SPLITS, UP_VAL_ROW_SPLITS + (o_ref.shape[0],)):
        rows = slice(start, stop)
        hv = jnp.dot(xe_ref[start:stop + HALO_ROWS, :], wv, preferred_element_type=jnp.float32)
        val = causal_conv(hv, cwv_ref[...], cbv_ref[...])
        acts.append((gate[rows] * val).astype(jnp.bfloat16))
    o_ref[...] = jnp.concatenate(acts, axis=0)
    wdb_ref[...] = wd_ref[...].astype(jnp.bfloat16)


def _ffn_up(xb, w_up, w_down, layer, cwg, cwv, cbg, cbv):
    seq = xb.shape[0]
    col = lambda i, j: (0, j)
    wd_cols = D_MODEL // (seq // UP_ROWS)
    val0 = D_FF // V7X_LANES
    last = 2 * D_FF // V7X_LANES - 1

    def val_index(m, i, j):
        return layer, 0, jnp.minimum(val0 + UP_VAL_BLOCKS * j + m, last)

    val_specs = [pl.BlockSpec((None, D_MODEL, V7X_LANES), functools.partial(val_index, m))
                 for m in range(UP_VAL_BLOCKS)]
    return pl.pallas_call(
        _ffn_up_kernel,
        grid=(seq // UP_ROWS, UP_TILES),
        in_specs=[pl.BlockSpec((UP_ROWS, D_MODEL), lambda i, j: (i, 0)),
                  pl.BlockSpec((HALO_ROWS, D_MODEL), _prev_halo_index(UP_ROWS)),
                  pl.BlockSpec((None, D_MODEL, UP_COLS), lambda i, j: (layer, 0, j)),
                  *val_specs,
                  pl.BlockSpec((CONV_WIDTH, UP_COLS), col),
                  pl.BlockSpec((CONV_WIDTH, UP_COLS), col),
                  pl.BlockSpec((1, UP_COLS), col),
                  pl.BlockSpec((1, UP_COLS), col),
                  pl.BlockSpec((None, UP_COLS, wd_cols), lambda i, j: (layer, j, i))],
        out_specs=[pl.BlockSpec((UP_ROWS, UP_COLS), lambda i, j: (i, j)),
                   pl.BlockSpec((UP_COLS, wd_cols), lambda i, j: (j, i))],
        out_shape=[jax.ShapeDtypeStruct((seq, FF_PAD), jnp.bfloat16),
                   jax.ShapeDtypeStruct((D_FF, D_MODEL), jnp.bfloat16)],
        scratch_shapes=[pltpu.VMEM((HALO_ROWS + UP_ROWS, D_MODEL), jnp.bfloat16)],
        compiler_params=_params("arbitrary", "arbitrary"),
        name="ffn_up",
    )(xb, xb, w_up, *([w_up] * UP_VAL_BLOCKS), cwg, cwv, cbg, cbv, w_down)


def _ffn_down_ln_kernel(a_ref, x_ref, w_ref, g_ref, b_ref, of_ref, ob_ref):
    k = pl.program_id(1)
    last = DOWN_STEPS - 1

    @pl.when(k == 0)
    def _():
        of_ref[...] = jnp.dot(a_ref[...], w_ref[...], preferred_element_type=jnp.float32)

    @pl.when((k > 0) & (k < last))
    def _():
        of_ref[...] += jnp.dot(a_ref[...], w_ref[...], preferred_element_type=jnp.float32)

    @pl.when(k == last)
    def _():
        kk = D_FF - last * DOWN_K
        w = w_ref[0:kk, :]
        half = a_ref.shape[0] // 2
        outs = []
        for r in range(2):
            rows = slice(r * half, (r + 1) * half)
            y = of_ref[rows, :] + jnp.dot(a_ref[rows, 0:kk], w, preferred_element_type=jnp.float32)
            outs.append(_residual_layer_norm(x_ref[rows, :], y, g_ref[...], b_ref[...]))
        out = jnp.concatenate(outs, axis=0)
        of_ref[...] = out
        ob_ref[...] = out.astype(jnp.bfloat16)


def _ffn_down_ln(act, x, w, g, b):
    seq = x.shape[0]
    row = lambda i, k: (i, 0)
    fixed = lambda i, k: (0, 0)
    return pl.pallas_call(
        _ffn_down_ln_kernel,
        grid=(seq // DOWN_ROWS, DOWN_STEPS),
        in_specs=[pl.BlockSpec((DOWN_ROWS, DOWN_K), lambda i, k: (i, k)),
                  pl.BlockSpec((DOWN_ROWS, D_MODEL), row),
                  pl.BlockSpec((DOWN_K, D_MODEL), lambda i, k: (k, 0)),
                  pl.BlockSpec((1, D_MODEL), fixed),
                  pl.BlockSpec((1, D_MODEL), fixed)],
        out_specs=[pl.BlockSpec((DOWN_ROWS, D_MODEL), row),
                   pl.BlockSpec((DOWN_ROWS, D_MODEL), row)],
        out_shape=[jax.ShapeDtypeStruct((seq, D_MODEL), jnp.float32),
                   jax.ShapeDtypeStruct((seq, D_MODEL), jnp.bfloat16)],
        compiler_params=_params("arbitrary", "arbitrary"),
        name="ffn_down_ln",
    )(act, x, w, g, b)


def _pad_ff(a, axis):
    pad = [(0, 0)] * a.ndim
    pad[axis] = (0, FF_PAD - D_FF)
    return jnp.pad(a, pad)


def kernel(x, pool_w, pool_scale, attn_w_qkv, attn_w_o, ffn_w_up, ffn_conv_w, ffn_conv_b,
           ffn_w_down, ln_mix_g, ln_mix_b, ln_ffn_g, ln_ffn_b):
    batch, seq, _ = x.shape
    bf16 = jnp.bfloat16
    vec = lambda a: a.reshape(1, -1)
    ntri = -jnp.tril(jnp.ones((ATTN_BLOCK, ATTN_BLOCK), jnp.float32), k=-1)

    outs = []
    for bi in range(batch):
        xf = x[bi]
        xb = None
        for i in range(DEPTH):
            j = i // 2
            g, b = vec(ln_mix_g[i]), vec(ln_mix_b[i])
            if i % 2 == 0:
                xf, xb = _pool_ln(xf, pool_w, j, vec(pool_scale[j]), g, b)
            else:
                qkv = _qkv(xb, attn_w_qkv, j)
                o = _attention(qkv, ntri)
                xf, xb = _proj_ln(o, xf, attn_w_o, j, g, b)
            cw, cb = ffn_conv_w[i], ffn_conv_b[i]
            act, w_down = _ffn_up(xb, ffn_w_up, ffn_w_down, i,
                          _pad_ff(cw[:, :D_FF], 1), _pad_ff(cw[:, D_FF:], 1),
                          _pad_ff(vec(cb[:D_FF]), 1), _pad_ff(vec(cb[D_FF:]), 1))
            xf, xb = _ffn_down_ln(act, xf, w_down, vec(ln_ffn_g[i]), vec(ln_ffn_b[i]))
        outs.append(xf)
    return jnp.stack(outs, axis=0)
```
